```python
import jax, jax.numpy as jnp
from jax import lax
import numpy as np

D_MODEL = 1024
BATCH = 8
SEQ = 2048
DEPTH = 4
DEC_BATCH = 32
DEC_SEQ = 8
PAST_LEN = 8192
PAGE_SIZE = 128

N_A = DEPTH // 2
N_B = DEPTH - N_A
CONV_W = 31
E_A = 2 * D_MODEL
HEAD_DIM = 64
N_HEADS = D_MODEL // HEAD_DIM
E_B = N_HEADS * HEAD_DIM
PLE_DIM = 256
Q_BLOCK = 128
F_BIAS_INIT = 3.0
EPS = 1e-6

kernel_name = 'yoco_conformer_fox_decode_step'


def rms_norm(x, g):
    x32 = x.astype(jnp.float32)
    y = x32 * lax.rsqrt(jnp.mean(x32 * x32, axis=-1, keepdims=True) + EPS)
    return y.astype(x.dtype) * g


def layer_norm(x, g, b):
    x32 = x.astype(jnp.float32)
    mu = jnp.mean(x32, axis=-1, keepdims=True)
    xc = x32 - mu
    y = xc * lax.rsqrt(jnp.mean(xc * xc, axis=-1, keepdims=True) + EPS)
    return y.astype(x.dtype) * g + b


def conv_mixer(x, prev, norm_g, w_in, conv_w, conv_b, ln_g, ln_b, w_out):
    h = rms_norm(x, norm_g)
    a, b, z = jnp.split(h @ w_in, 3, axis=-1)
    u = a * jax.nn.sigmoid(b)
    u_ext = jnp.concatenate([prev.astype(u.dtype), u], axis=1)
    c = lax.conv_general_dilated(u_ext, conv_w[:, None, :], window_strides=(1,), padding='VALID',
                                 dimension_numbers=('NWC', 'WIO', 'NWC'),
                                 feature_group_count=E_A) + conv_b
    c = jax.nn.silu(layer_norm(c, ln_g, ln_b))
    out = (c * jax.nn.silu(z)) @ w_out
    return out, u_ext[:, -(CONV_W - 1):]


def shared_kv(x, kv_norm, kv_w, k_norm, f_bias):
    bsz, s = x.shape[:2]
    h = rms_norm(x, kv_norm)
    proj = h @ kv_w
    k = proj[..., :E_B].reshape(bsz, s, N_HEADS, HEAD_DIM)
    v = proj[..., E_B:2 * E_B].reshape(bsz, s, N_HEADS, HEAD_DIM)
    fl = proj[..., 2 * E_B:] + f_bias
    k = rms_norm(k, k_norm)
    logf = jax.nn.log_sigmoid(fl.astype(jnp.float32))
    return k, v, logf


def fox_attend(q, k, v, cq, ck, q_pos, k_pos):
    s = jnp.einsum('bqhd,bkhd->bhqk', q, k).astype(jnp.float32) * (HEAD_DIM ** -0.5)
    s = s + (jnp.transpose(cq, (0, 2, 1))[..., :, None] - jnp.transpose(ck, (0, 2, 1))[..., None, :])
    mask = k_pos[None, :] <= q_pos[:, None]
    s = jnp.where(mask, s, jnp.finfo(jnp.float32).min)
    p = jax.nn.softmax(s, axis=-1).astype(v.dtype)
    return jnp.einsum('bhqk,bkhd->bqhd', p, v)


def fox_prompt(q, k, v, c):
    bsz, s = q.shape[:2]
    n_blocks = s // Q_BLOCK
    k_pos = jnp.arange(s, dtype=jnp.int32)

    def block(i):
        st = i * Q_BLOCK
        qb = lax.dynamic_slice_in_dim(q, st, Q_BLOCK, axis=1)
        cb = lax.dynamic_slice_in_dim(c, st, Q_BLOCK, axis=1)
        qp = st + jnp.arange(Q_BLOCK, dtype=jnp.int32)
        return fox_attend(qb, k, v, cb, c, qp, k_pos)

    out = lax.map(block, jnp.arange(n_blocks, dtype=jnp.int32))
    return jnp.transpose(out, (1, 0, 2, 3, 4)).reshape(bsz, s, N_HEADS, HEAD_DIM)


def fox_layer(x, attend, norm_g, w_q, q_norm, w_out):
    bsz, s = x.shape[:2]
    h = rms_norm(x, norm_g)
    qz = h @ w_q
    q = rms_norm(qz[..., :E_B].reshape(bsz, s, N_HEADS, HEAD_DIM), q_norm)
    z = qz[..., E_B:]
    o = attend(q).reshape(bsz, s, E_B)
    return (o * jax.nn.silu(z)) @ w_out


def run_group(x, p, conv_prev, past_k, past_v, past_logf,
              a_norm, a_w_in, a_conv_w, a_conv_b, a_ln_g, a_ln_b, a_w_out,
              kv_norm, kv_w, kv_k_norm, kv_f_bias,
              b_norm, b_w_q, b_q_norm, b_w_out, ple_w, ple_gate_w):
    t = x.shape[1]
    conv_states = []
    k_new = v_new = logf_new = None
    attend = None
    for i in range(DEPTH):
        if i < N_A:
            o, st = conv_mixer(x, conv_prev[i], a_norm[i], a_w_in[i], a_conv_w[i], a_conv_b[i],
                               a_ln_g[i], a_ln_b[i], a_w_out[i])
            conv_states.append(st)
        else:
            if i == N_A:
                k_new, v_new, logf_new = shared_kv(x, kv_norm, kv_w, kv_k_norm, kv_f_bias)
                if past_k is None:
                    c_all = jnp.cumsum(logf_new, axis=1)
                    attend = lambda q, k=k_new, v=v_new, c=c_all: fox_prompt(q, k, v, c)
                else:
                    p_len = past_k.shape[1]
                    k_all = jnp.concatenate([past_k.astype(k_new.dtype), k_new], axis=1)
                    v_all = jnp.concatenate([past_v.astype(v_new.dtype), v_new], axis=1)
                    c_all = jnp.cumsum(jnp.concatenate([past_logf.astype(jnp.float32), logf_new], axis=1), axis=1)
                    q_pos = p_len + jnp.arange(t, dtype=jnp.int32)
                    k_pos = jnp.arange(p_len + t, dtype=jnp.int32)
                    attend = lambda q, k=k_all, v=v_all, c=c_all, qp=q_pos, kp=k_pos, pl=p_len: fox_attend(q, k, v, c[:, pl:], c, qp, kp)
            j = i - N_A
            o = fox_layer(x, attend, b_norm[j], b_w_q[j], b_q_norm[j], b_w_out[j])
        x = x + o
        x = x + jax.nn.sigmoid(x @ ple_gate_w[i]) * (p[i].astype(x.dtype) @ ple_w[i])
    return x, jnp.stack(conv_states, axis=0), k_new, v_new, logf_new


def setup_inputs(seed: int = 0) -> dict:
    key = jax.random.key(seed)
    ks = jax.random.split(key, 32)
    n_pages = PAST_LEN // PAGE_SIZE
    n_used = DEC_BATCH * n_pages
    n_pool = n_used + max(1, n_used // 4)
    f32 = jnp.float32

    def nrm(k, shape, scale):
        return jax.random.normal(k, shape, f32) * scale

    perm = jax.random.permutation(ks[0], n_pool)
    page_table = perm[:n_used].reshape(DEC_BATCH, n_pages).astype(jnp.int32)
    return {
        'x_prompt': nrm(ks[1], (BATCH, SEQ, D_MODEL), 1.0),
        'x_sample': nrm(ks[2], (DEC_BATCH, DEC_SEQ, D_MODEL), 1.0),
        'state_conv': nrm(ks[3], (N_A, DEC_BATCH, CONV_W - 1, E_A), 1.0),
        'cache_k': nrm(ks[4], (n_pool, PAGE_SIZE, N_HEADS, HEAD_DIM), 1.0),
        'cache_v': nrm(ks[5], (n_pool, PAGE_SIZE, N_HEADS, HEAD_DIM), 1.0),
        'cache_logf': jax.nn.log_sigmoid(F_BIAS_INIT + nrm(ks[6], (n_pool, PAGE_SIZE, N_HEADS), 1.0)),
        'page_table': page_table,
        'p_prompt': nrm(ks[7], (DEPTH, BATCH, SEQ, PLE_DIM), 1.0),
        'p_sample': nrm(ks[8], (DEPTH, DEC_BATCH, DEC_SEQ, PLE_DIM), 1.0),
        'a_norm': 1.0 + nrm(ks[9], (N_A, D_MODEL), 0.02),
        'a_w_in': nrm(ks[10], (N_A, D_MODEL, 3 * E_A), D_MODEL ** -0.5),
        'a_conv_w': nrm(ks[11], (N_A, CONV_W, E_A), CONV_W ** -0.5),
        'a_conv_b': nrm(ks[12], (N_A, E_A), 0.02),
        'a_ln_g': 1.0 + nrm(ks[13], (N_A, E_A), 0.02),
        'a_ln_b': nrm(ks[14], (N_A, E_A), 0.02),
        'a_w_out': nrm(ks[15], (N_A, E_A, D_MODEL), E_A ** -0.5),
        'kv_norm': 1.0 + nrm(ks[16], (D_MODEL,), 0.02),
        'kv_w': nrm(ks[17], (D_MODEL, 2 * E_B + N_HEADS), D_MODEL ** -0.5),
        'kv_k_norm': 1.0 + nrm(ks[18], (HEAD_DIM,), 0.02),
        'kv_f_bias': F_BIAS_INIT + nrm(ks[19], (N_HEADS,), 0.1),
        'b_norm': 1.0 + nrm(ks[20], (N_B, D_MODEL), 0.02),
        'b_w_q': nrm(ks[21], (N_B, D_MODEL, 2 * E_B), D_MODEL ** -0.5),
        'b_q_norm': 1.0 + nrm(ks[22], (N_B, HEAD_DIM), 0.02),
        'b_w_out': nrm(ks[23], (N_B, E_B, D_MODEL), E_B ** -0.5),
        'ple_w': nrm(ks[24], (DEPTH, PLE_DIM, D_MODEL), PLE_DIM ** -0.5),
        'ple_gate_w': nrm(ks[25], (DEPTH, D_MODEL, D_MODEL), D_MODEL ** -0.5),
    }


def reference(x_prompt, x_sample, state_conv, cache_k, cache_v, cache_logf, page_table,
              p_prompt, p_sample,
              a_norm, a_w_in, a_conv_w, a_conv_b, a_ln_g, a_ln_b, a_w_out,
              kv_norm, kv_w, kv_k_norm, kv_f_bias,
              b_norm, b_w_q, b_q_norm, b_w_out, ple_w, ple_gate_w):
    db, n_pages = page_table.shape

    def gather(pool):
        g = pool[page_table]
        return g.reshape((db, n_pages * pool.shape[1]) + pool.shape[2:])

    conv0 = jnp.zeros((N_A, x_prompt.shape[0], CONV_W - 1, E_A), x_prompt.dtype)
    y_prompt, conv_p, k_p, v_p, lf_p = run_group(
        x_prompt, p_prompt, conv0, None, None, None,
        a_norm, a_w_in, a_conv_w, a_conv_b, a_ln_g, a_ln_b, a_w_out,
        kv_norm, kv_w, kv_k_norm, kv_f_bias,
        b_norm, b_w_q, b_q_norm, b_w_out, ple_w, ple_gate_w)
    y_sample, conv_s, k_s, v_s, lf_s = run_group(
        x_sample, p_sample, state_conv, gather(cache_k), gather(cache_v), gather(cache_logf),
        a_norm, a_w_in, a_conv_w, a_conv_b, a_ln_g, a_ln_b, a_w_out,
        kv_norm, kv_w, kv_k_norm, kv_f_bias,
        b_norm, b_w_q, b_q_norm, b_w_out, ple_w, ple_gate_w)
    return (y_prompt, y_sample, conv_p, k_p, v_p, lf_p, conv_s, k_s, v_s, lf_s)
```

```python
import functools
import math

import numpy as np
import jax
import jax.numpy as jnp
from jax import lax
from jax.experimental import pallas as pl
from jax.experimental.pallas import tpu as pltpu

F32 = jnp.float32
BF16 = jnp.bfloat16

EPS = 1e-6
CONV_W = 31
HALO = CONV_W - 1
HEAD_DIM = 64
LANES = 128
SUBLANES = 8
HALO_PAD = 32
LOG2E = math.log2(math.e)
NEG_BIG = -1e30
VMEM_LIMIT = 56 * 1024 * 1024


def _dot(a, b):
    return jnp.dot(a, b, preferred_element_type=F32)


def _dot_nt(a, b):
    return lax.dot_general(a, b, (((1,), (1,)), ((), ())), preferred_element_type=F32)


def _split3(x):
    p0 = x.astype(BF16)
    r = x - p0.astype(F32)
    p1 = r.astype(BF16)
    r = r - p1.astype(F32)
    return p0, p1, r.astype(BF16)


def _split2(x):
    p0 = x.astype(BF16)
    return p0, (x - p0.astype(F32)).astype(BF16)


def _rms(x, g):
    return x * lax.rsqrt(jnp.mean(x * x, axis=-1, keepdims=True) + EPS) * g


def _sigmoid(x):
    return 1.0 / (1.0 + jnp.exp(-x))


def _silu(x):
    return x * _sigmoid(x)


def _log_sigmoid(x):
    return jnp.minimum(x, 0.0) - jnp.log1p(jnp.exp(-jnp.abs(x)))


def _const_spec(shape):
    nd = len(shape)
    return pl.BlockSpec(shape, lambda *_: (0,) * nd, pipeline_mode=pl.Buffered(1))


def _params(sem):
    return pltpu.CompilerParams(dimension_semantics=sem, vmem_limit_bytes=VMEM_LIMIT)


def _ple_tail(x1, p, gw_ref, pw_ref):
    gate = _sigmoid(_dot(x1.astype(BF16), gw_ref[...]))
    return x1 + gate * _dot(p.astype(BF16), pw_ref[...])


def _conv_layer_kernel(*refs, nseq, t, seq_chunk, row_chunk, lane_chunk, has_prev):
    if has_prev:
        (x_ref, p_ref, prev_ref, g_ref, win_ref, cw_ref, cb_ref, lng_ref, lnb_ref, wout_ref,
         gw_ref, pw_ref, y_ref, st_ref, uext_ref, c_ref) = refs
    else:
        (x_ref, p_ref, g_ref, win_ref, cw_ref, cb_ref, lng_ref, lnb_ref, wout_ref,
         gw_ref, pw_ref, y_ref, st_ref, uext_ref, c_ref) = refs
    e = cw_ref.shape[1]
    m = nseq * t
    x = x_ref[...].reshape(m, x_ref.shape[-1])

    if has_prev:
        uext_ref[:, pl.ds(HALO_PAD - HALO, HALO), :] = prev_ref[...]
    else:
        s = pl.program_id(1)

        @pl.when(s == 0)
        def _():
            uext_ref[:, pl.ds(0, HALO_PAD), :] = jnp.zeros((nseq, HALO_PAD, e), F32)

        @pl.when(s > 0)
        def _():
            uext_ref[:, pl.ds(0, HALO_PAD), :] = uext_ref[:, pl.ds(t, HALO_PAD), :]

    h = _rms(x, g_ref[...]).astype(BF16)
    a = _dot(h, win_ref[:, pl.ds(0, e)])
    b = _dot(h, win_ref[:, pl.ds(e, e)])
    uext_ref[:, pl.ds(HALO_PAD, t), :] = (a * _sigmoid(b)).reshape(nseq, t, e)

    for n0 in range(0, nseq, seq_chunk):
        for r0 in range(0, t, row_chunk):
            for l0 in range(0, e, lane_chunk):
                acc = jnp.broadcast_to(cb_ref[:, pl.ds(l0, lane_chunk)].reshape(1, 1, lane_chunk),
                                       (seq_chunk, row_chunk, lane_chunk))
                for k in range(CONV_W):
                    win = uext_ref[pl.ds(n0, seq_chunk), pl.ds(HALO_PAD - HALO + r0 + k, row_chunk),
                                   pl.ds(l0, lane_chunk)]
                    acc = acc + win * cw_ref[pl.ds(k, 1), pl.ds(l0, lane_chunk)].reshape(1, 1, lane_chunk)
                c_ref[pl.ds(n0 * t + r0, seq_chunk * row_chunk), pl.ds(l0, lane_chunk)] = (
                    acc.reshape(seq_chunk * row_chunk, lane_chunk))

    if has_prev:
        st_ref[...] = uext_ref[:, pl.ds(HALO_PAD + t - HALO, HALO), :]
    else:
        @pl.when(pl.program_id(1) == pl.num_programs(1) - 1)
        def _():
            st_ref[...] = uext_ref[:, pl.ds(HALO_PAD + t - HALO, HALO), :]

    c = c_ref[...]
    mu = jnp.mean(c, axis=-1, keepdims=True)
    cc = c - mu
    ln = cc * lax.rsqrt(jnp.mean(cc * cc, axis=-1, keepdims=True) + EPS) * lng_ref[...] + lnb_ref[...]
    z = _dot(h, win_ref[:, pl.ds(2 * e, e)])
    gated = (_silu(ln) * _silu(z)).astype(BF16)
    x1 = x + _dot(gated, wout_ref[...])
    y = _ple_tail(x1, p_ref[...].reshape(m, p_ref.shape[-1]), gw_ref, pw_ref)
    y_ref[...] = y.reshape(y_ref.shape)


def _conv_layer(x, p_all, layer, prev, g, w_in, conv_w, conv_b, ln_g, ln_b, w_out, gate_w, ple_w, *, tm):
    bsz, s, d = x.shape
    e = conv_w.shape[1]
    pd = p_all.shape[-1]
    has_prev = prev is not None
    weights = (g, w_in, conv_w, conv_b, ln_g, ln_b, w_out, gate_w, ple_w)
    wspecs = [_const_spec(w.shape) for w in weights]
    if has_prev:
        assert s % SUBLANES == 0
        nseq, t, grid = bsz, s, (1,)
        x_spec = pl.BlockSpec((bsz, s, d), lambda i: (0, 0, 0))
        p_spec = pl.BlockSpec((None, bsz, s, pd), lambda i: (layer, 0, 0, 0))
        in_specs = [x_spec, p_spec, pl.BlockSpec(prev.shape, lambda i: (0, 0, 0))] + wspecs
        out_specs = [x_spec, pl.BlockSpec((bsz, HALO, e), lambda i: (0, 0, 0))]
        args = (x, p_all, prev) + weights
        sem = ("arbitrary",)
        seq_chunk, row_chunk = 4, s
        assert bsz % seq_chunk == 0
    else:
        assert s % tm == 0 and tm >= HALO_PAD and tm % HALO_PAD == 0
        nseq, t, grid = 1, tm, (bsz, s // tm)
        x_spec = pl.BlockSpec((1, tm, d), lambda b, i: (b, i, 0))
        p_spec = pl.BlockSpec((None, 1, tm, pd), lambda b, i: (layer, b, i, 0))
        in_specs = [x_spec, p_spec] + wspecs
        out_specs = [x_spec, pl.BlockSpec((1, HALO, e), lambda b, i: (b, 0, 0))]
        args = (x, p_all) + weights
        sem = ("arbitrary", "arbitrary")
        seq_chunk, row_chunk = 1, HALO_PAD
    kern = functools.partial(_conv_layer_kernel, nseq=nseq, t=t, seq_chunk=seq_chunk,
                             row_chunk=row_chunk, lane_chunk=4 * LANES, has_prev=has_prev)
    return pl.pallas_call(
        kern,
        grid=grid,
        in_specs=in_specs,
        out_specs=out_specs,
        out_shape=[jax.ShapeDtypeStruct((bsz, s, d), F32), jax.ShapeDtypeStruct((bsz, HALO, e), F32)],
        scratch_shapes=[pltpu.VMEM((nseq, HALO_PAD + t, e), F32), pltpu.VMEM((nseq * t, e), F32)],
        compiler_params=_params(sem),
        name="conv_layer_decode" if has_prev else "conv_layer_prompt",
    )(*args)


def _head_norm(raw, bd_ref, gain):
    hi, lo = _split2(raw * raw)
    ms = (_dot(hi, bd_ref[...]) + _dot(lo, bd_ref[...])) * (1.0 / HEAD_DIM)
    return raw * lax.rsqrt(ms + EPS) * gain


def _head_tiles(arr, aug, n_heads, lane):
    for hd in range(n_heads):
        pair = arr[:, (hd // 2) * LANES:(hd // 2 + 1) * LANES]
        if hd % 2 == 1:
            pair = pltpu.roll(pair, HEAD_DIM, axis=1)
        yield hd, jnp.where(lane < HEAD_DIM, pair, aug[:, hd * LANES:(hd + 1) * LANES])


def _kv_kernel(*refs, n_heads, prompt):
    if prompt:
        (x_ref, g_ref, wkv_ref, wf_ref, fb_ref, kg_ref, bd_ref, sel_ref, ones_ref,
         k_ref, v_ref, lf_ref, c_ref, kp_ref, vp_ref, carry_ref) = refs
    else:
        (x_ref, g_ref, wkv_ref, wf_ref, fb_ref, kg_ref, bd_ref, k_ref, v_ref, lf_ref) = refs
    e_b = n_heads * HEAD_DIM
    m = x_ref.shape[-2]
    x = x_ref[...].reshape(m, x_ref.shape[-1])
    h = _rms(x, g_ref[...]).astype(BF16)
    kn = _head_norm(_dot(h, wkv_ref[:, pl.ds(0, e_b)]), bd_ref, kg_ref[...])
    v = _dot(h, wkv_ref[:, pl.ds(e_b, e_b)])
    logf = _log_sigmoid(_dot(h, wf_ref[...]) + fb_ref[...])
    k_ref[...] = kn.reshape(k_ref.shape)
    v_ref[...] = v.reshape(v_ref.shape)
    lf_ref[...] = logf[:, :n_heads].reshape(lf_ref.shape)
    if not prompt:
        return

    @pl.when(pl.program_id(1) == 0)
    def _():
        carry_ref[...] = jnp.zeros(carry_ref.shape, F32)

    row = lax.broadcasted_iota(jnp.int32, (m, m), 0)
    col = lax.broadcasted_iota(jnp.int32, (m, m), 1)
    tri = jnp.where(col <= row, 1.0, 0.0).astype(BF16)
    c = carry_ref[...]
    for piece in _split3(logf * LOG2E):
        c = c + _dot(tri, piece)
    carry_ref[...] = c[m - 1:m, :]
    c_ref[...] = c.reshape(c_ref.shape)

    aug = ones_ref[...]
    for i, piece in enumerate(_split3(-c)):
        aug = aug + _dot(piece, sel_ref[i])
    lane = lax.broadcasted_iota(jnp.int32, (m, LANES), 1)
    for hd, tile in _head_tiles(kn, aug, n_heads, lane):
        kp_ref[0, hd] = tile.astype(BF16)
    vaug = jnp.broadcast_to(jnp.where(lane == HEAD_DIM, 1.0, 0.0), (m, LANES))
    vaug = jnp.concatenate([vaug] * n_heads, axis=1)
    for hd, tile in _head_tiles(v, vaug, n_heads, lane):
        vp_ref[0, hd] = tile.astype(BF16)


def _sel_mats(n_heads, first_lane):
    sel = np.zeros((3, LANES, n_heads * LANES), np.float32)
    for i in range(3):
        for hd in range(n_heads):
            sel[i, hd, hd * LANES + first_lane + i] = 1.0
    return jnp.asarray(sel, BF16)


def _ones_row(n_heads, first_lane):
    row = np.zeros((1, n_heads * LANES), np.float32)
    for hd in range(n_heads):
        row[0, hd * LANES + first_lane:hd * LANES + first_lane + 3] = 1.0
    return jnp.asarray(row)


def _block_diag_ones(n_heads):
    idx = np.arange(n_heads * HEAD_DIM) // HEAD_DIM
    return jnp.asarray((idx[:, None] == idx[None, :]).astype(np.float32), BF16)


def _kv_proj(x, g, w_kv, w_f, f_bias, k_gain, bd, *, n_heads, prompt, tm):
    bsz, s, d = x.shape
    e_b = n_heads * HEAD_DIM
    outs = [jax.ShapeDtypeStruct((bsz, s, e_b), F32), jax.ShapeDtypeStruct((bsz, s, e_b), F32),
            jax.ShapeDtypeStruct((bsz, s, n_heads), F32)]
    weights = [g, w_kv, w_f, f_bias, k_gain, bd]
    if prompt:
        assert s % tm == 0
        grid = (bsz, s // tm)
        weights += [_sel_mats(n_heads, HEAD_DIM), _ones_row(n_heads, HEAD_DIM + 3)]
        tok = lambda width: pl.BlockSpec((1, tm, width), lambda b, i: (b, i, 0))
        head_spec = pl.BlockSpec((1, n_heads, tm, LANES), lambda b, i: (b, 0, i, 0))
        out_specs = [tok(e_b), tok(e_b), tok(n_heads), tok(LANES), head_spec, head_spec]
        outs += [jax.ShapeDtypeStruct((bsz, s, LANES), F32),
                 jax.ShapeDtypeStruct((bsz, n_heads, s, LANES), BF16),
                 jax.ShapeDtypeStruct((bsz, n_heads, s, LANES), BF16)]
        x_spec = tok(d)
        scratch = [pltpu.VMEM((1, LANES), F32)]
        sem = ("arbitrary", "arbitrary")
    else:
        grid = (1,)
        x2 = x.reshape(1, bsz * s, d)
        full = lambda width: pl.BlockSpec((1, bsz * s, width), lambda i: (0, 0, 0))
        outs = [jax.ShapeDtypeStruct((1, bsz * s, o.shape[-1]), F32) for o in outs]
        out_specs = [full(e_b), full(e_b), full(n_heads)]
        x_spec = full(d)
        x = x2
        scratch = []
        sem = ("arbitrary",)
    res = pl.pallas_call(
        functools.partial(_kv_kernel, n_heads=n_heads, prompt=prompt),
        grid=grid,
        in_specs=[x_spec] + [_const_spec(w.shape) for w in weights],
        out_specs=out_specs,
        out_shape=outs,
        scratch_shapes=scratch,
        compiler_params=_params(sem),
        name="kv_proj_prompt" if prompt else "kv_proj_decode",
    )(x, *weights)
    if not prompt:
        res = [r.reshape(bsz, s, r.shape[-1]) for r in res]
    return res


def _fox_q_kernel(*refs, n_heads, prompt):
    if prompt:
        x_ref, c_ref, g_ref, wq_ref, qg_ref, bd_ref, sel_ref, ones_ref, qp_ref, zs_ref = refs
    else:
        x_ref, g_ref, wq_ref, qg_ref, bd_ref, qn_ref, zs_ref = refs
    e_b = n_heads * HEAD_DIM
    m = x_ref.shape[-2]
    x = x_ref[...].reshape(m, x_ref.shape[-1])
    h = _rms(x, g_ref[...]).astype(BF16)
    qn = _head_norm(_dot(h, wq_ref[:, pl.ds(0, e_b)]), bd_ref, qg_ref[...])
    zs_ref[...] = _silu(_dot(h, wq_ref[:, pl.ds(e_b, e_b)])).reshape(zs_ref.shape)
    if not prompt:
        qn_ref[...] = qn.reshape(qn_ref.shape)
        return
    aug = ones_ref[...]
    for i, piece in enumerate(_split3(c_ref[...].reshape(m, LANES))):
        aug = aug + _dot(piece, sel_ref[i])
    lane = lax.broadcasted_iota(jnp.int32, (m, LANES), 1)
    for hd, tile in _head_tiles(qn, aug, n_heads, lane):
        qp_ref[0, hd] = tile.astype(BF16)


def _fox_q(x, c, g, w_q, q_gain, bd, *, n_heads, prompt, tm):
    bsz, s, d = x.shape
    e_b = n_heads * HEAD_DIM
    if prompt:
        grid = (bsz, s // tm)
        tok = lambda width: pl.BlockSpec((1, tm, width), lambda b, i: (b, i, 0))
        weights = [g, w_q, q_gain, bd, _sel_mats(n_heads, HEAD_DIM + 3), _ones_row(n_heads, HEAD_DIM)]
        args = (x, c)
        in_specs = [tok(d), tok(LANES)]
        out_specs = [pl.BlockSpec((1, n_heads, tm, LANES), lambda b, i: (b, 0, i, 0)), tok(e_b)]
        outs = [jax.ShapeDtypeStruct((bsz, n_heads, s, LANES), BF16), jax.ShapeDtypeStruct((bsz, s, e_b), F32)]
        sem = ("arbitrary", "arbitrary")
    else:
        grid = (1,)
        full = lambda width: pl.BlockSpec((1, bsz * s, width), lambda i: (0, 0, 0))
        weights = [g, w_q, q_gain, bd]
        args = (x.reshape(1, bsz * s, d),)
        in_specs = [full(d)]
        out_specs = [full(e_b), full(e_b)]
        outs = [jax.ShapeDtypeStruct((1, bsz * s, e_b), F32)] * 2
        sem = ("arbitrary",)
    res = pl.pallas_call(
        functools.partial(_fox_q_kernel, n_heads=n_heads, prompt=prompt),
        grid=grid,
        in_specs=in_specs + [_const_spec(w.shape) for w in weights],
        out_specs=out_specs,
        out_shape=outs,
        compiler_params=_params(sem),
        name="fox_q_prompt" if prompt else "fox_q_decode",
    )(*args, *weights)
    if not prompt:
        res = [r.reshape(bsz, s, e_b) for r in res]
    return res


def _attn_prompt_kernel(q_ref, k_ref, v_ref, o_ref, *, tq):
    i = pl.program_id(2)
    row = lax.broadcasted_iota(jnp.int32, (tq, tq), 0)
    col = lax.broadcasted_iota(jnp.int32, (tq, tq), 1)
    lane = lax.broadcasted_iota(jnp.int32, (tq, LANES), 1)
    outs = []
    for hh in range(2):
        q = q_ref[0, hh]

        def step(j, carry, masked, q=q, hh=hh):
            m_old, acc = carry
            start = pl.multiple_of(j * tq, tq)
            s = _dot_nt(q, k_ref[0, hh, pl.ds(start, tq), :])
            if masked:
                s = jnp.where(col <= row, s, NEG_BIG)
            m_new = jnp.maximum(m_old, jnp.max(s, axis=-1, keepdims=True))
            p = jnp.exp2(s - m_new).astype(BF16)
            acc = jnp.exp2(m_old - m_new) * acc + _dot(p, v_ref[0, hh, pl.ds(start, tq), :])
            return m_new, acc

        init = (jnp.full((tq, 1), NEG_BIG, F32), jnp.zeros((tq, LANES), F32))
        carry = lax.fori_loop(0, i, functools.partial(step, masked=False), init)
        _, acc = step(i, carry, True)
        denom = jnp.sum(jnp.where(lane == HEAD_DIM, acc, 0.0), axis=-1, keepdims=True)
        outs.append(acc / denom)
    o_ref[0] = jnp.where(lane < HEAD_DIM, outs[0], pltpu.roll(outs[1], HEAD_DIM, axis=1))


def _attn_prompt(qp, kp, vp, *, tq):
    bsz, n_heads, s, _ = qp.shape
    assert s % tq == 0 and n_heads % 2 == 0
    kv_spec = pl.BlockSpec((1, 2, s, LANES), lambda b, hp, i: (b, hp, 0, 0))
    return pl.pallas_call(
        functools.partial(_attn_prompt_kernel, tq=tq),
        grid=(bsz, n_heads // 2, s // tq),
        in_specs=[pl.BlockSpec((1, 2, tq, LANES), lambda b, hp, i: (b, hp, i, 0)), kv_spec, kv_spec],
        out_specs=pl.BlockSpec((1, tq, LANES), lambda b, hp, i: (b, i, hp)),
        out_shape=jax.ShapeDtypeStruct((bsz, s, n_heads * HEAD_DIM), F32),
        compiler_params=_params(("arbitrary", "arbitrary", "arbitrary")),
        name="attn_prompt",
    )(qp, kp, vp)


def _attn_decode_kernel(*refs, n_pages_step, n_heads, t):
    g = n_pages_step
    qn_ref, kn_ref, vn_ref, lfn_ref, exp_ref, rep_ref = refs[1:7]
    k_refs = refs[7:7 + g]
    v_refs = refs[7 + g:7 + 2 * g]
    lf_refs = refs[7 + 2 * g:7 + 3 * g]
    o_ref = refs[7 + 3 * g]
    qexp_ref, m_ref, l_ref, acc_ref, carry_ref = refs[8 + 3 * g:]
    ht = n_heads * t
    e_b = n_heads * HEAD_DIM
    page = k_refs[0].shape[1]
    s_idx = pl.program_id(1)

    @pl.when(s_idx == 0)
    def _():
        spread = _dot(rep_ref[...], qn_ref[0].astype(BF16).astype(F32))
        keep = (lax.broadcasted_iota(jnp.int32, (ht, e_b), 0) // t
                == lax.broadcasted_iota(jnp.int32, (ht, e_b), 1) // HEAD_DIM)
        qexp_ref[...] = jnp.where(keep, spread, 0.0).astype(BF16)
        m_ref[...] = jnp.full(m_ref.shape, NEG_BIG, F32)
        l_ref[...] = jnp.zeros(l_ref.shape, F32)
        acc_ref[...] = jnp.zeros(acc_ref.shape, F32)
        carry_ref[...] = jnp.zeros(carry_ref.shape, F32)

    row = lax.broadcasted_iota(jnp.int32, (page, page), 0)
    col = lax.broadcasted_iota(jnp.int32, (page, page), 1)
    upper = jnp.where(row <= col, 1.0, 0.0).astype(BF16)

    def rows_per_head(a):
        return jnp.broadcast_to(a[:, None, :], (n_heads, t, a.shape[-1])).reshape(ht, a.shape[-1])

    def cumsum_keys(c, lf):
        for piece in _split3(lf):
            c = c + _dot(piece, upper)
        return c

    def page_bias(lf_t):
        c = cumsum_keys(jnp.broadcast_to(carry_ref[...], (n_heads, page)), lf_t * LOG2E)
        carry_ref[...] = c[:, page - 1:page]
        return c

    def update(s, pv):
        m_old = m_ref[...]
        m_new = jnp.maximum(m_old, jnp.max(s, axis=1, keepdims=True))
        p = jnp.exp2(s - m_new)
        alpha = jnp.exp2(m_old - m_new)
        m_ref[...] = m_new
        l_ref[...] = alpha * l_ref[...] + jnp.sum(p, axis=1, keepdims=True)
        acc_ref[...] = alpha * acc_ref[...] + pv(p.astype(BF16))

    qexp = qexp_ref[...]
    s = jnp.concatenate([_dot(qexp, k_refs[i][...].astype(BF16)) - rows_per_head(page_bias(lf_refs[i][...]))
                         for i in range(g)], axis=1)
    update(s, lambda p: sum(_dot_nt(p[:, i * page:(i + 1) * page], v_refs[i][...].astype(BF16))
                            for i in range(g)))

    @pl.when(s_idx == pl.num_programs(1) - 1)
    def _():
        pad = lambda a: jnp.concatenate([a, jnp.zeros((page - t, a.shape[1]), F32)], axis=0)
        lexp = jnp.zeros((page, ht), F32)
        for piece in _split3(pad(lfn_ref[0]) * LOG2E):
            lexp = lexp + _dot(piece, exp_ref[...])
        c = cumsum_keys(rows_per_head(jnp.broadcast_to(carry_ref[...], (n_heads, page))), lexp.T)
        s_new = _dot_nt(qexp, pad(kn_ref[0]).astype(BF16)) - c
        key = lax.broadcasted_iota(jnp.int32, (ht, page), 1)
        tok = lax.broadcasted_iota(jnp.int32, (ht, page), 0) % t
        v_pad = pad(vn_ref[0]).astype(BF16)
        update(jnp.where(key <= tok, s_new, NEG_BIG), lambda p: _dot(p, v_pad))
        keep = (lax.broadcasted_iota(jnp.int32, (ht, e_b), 0) // t
                == lax.broadcasted_iota(jnp.int32, (ht, e_b), 1) // HEAD_DIM)
        o = jnp.where(keep, acc_ref[...] / l_ref[...], 0.0)
        o_ref[0] = jnp.sum(o.reshape(n_heads, t, e_b), axis=0)


def _attn_decode(qn, k_new, v_new, lf_new, cache_k, cache_v, cache_logf, page_table, *, n_heads, pages_per_step):
    bsz, t, e_b = qn.shape
    n_pool, page, _, _ = cache_k.shape
    n_pages = page_table.shape[1]
    g = pages_per_step
    ht = n_heads * t
    assert n_pages % g == 0 and ht == page == LANES and t % SUBLANES == 0
    ck = jnp.transpose(cache_k, (0, 2, 3, 1)).reshape(n_pool, e_b, page)
    cv = jnp.transpose(cache_v, (0, 2, 3, 1)).reshape(n_pool, e_b, page)
    clf = jnp.transpose(cache_logf, (0, 2, 1))
    expand = jnp.asarray((np.arange(n_heads)[:, None] == (np.arange(ht)[None, :] // t)).astype(np.float32), BF16)
    repeat = jnp.asarray((np.arange(ht)[:, None] % t == np.arange(t)[None, :]).astype(np.float32))
    tok = lambda width: pl.BlockSpec((1, t, width), lambda b, s, pt: (b, 0, 0))
    page_spec = lambda rows, i: pl.BlockSpec((None, rows, page), lambda b, s, pt, i=i: (pt[b, s * g + i], 0, 0))
    in_specs = ([tok(e_b), tok(e_b), tok(e_b), tok(n_heads),
                 pl.BlockSpec(expand.shape, lambda b, s, pt: (0, 0)),
                 pl.BlockSpec(repeat.shape, lambda b, s, pt: (0, 0))]
                + [page_spec(e_b, i) for i in range(g)] * 2
                + [page_spec(n_heads, i) for i in range(g)])
    return pl.pallas_call(
        functools.partial(_attn_decode_kernel, n_pages_step=g, n_heads=n_heads, t=t),
        grid_spec=pltpu.PrefetchScalarGridSpec(
            num_scalar_prefetch=1,
            grid=(bsz, n_pages // g),
            in_specs=in_specs,
            out_specs=tok(e_b),
            scratch_shapes=[pltpu.VMEM((ht, e_b), BF16), pltpu.VMEM((ht, 1), F32), pltpu.VMEM((ht, 1), F32),
                            pltpu.VMEM((ht, e_b), F32), pltpu.VMEM((n_heads, 1), F32)]),
        out_shape=jax.ShapeDtypeStruct((bsz, t, e_b), F32),
        compiler_params=_params(("arbitrary", "arbitrary")),
        name="attn_decode",
    )(page_table, qn, k_new, v_new, lf_new, expand, repeat, *([ck] * g), *([cv] * g), *([clf] * g))


def _fox_out_kernel(o_ref, zs_ref, x_ref, p_ref, wout_ref, gw_ref, pw_ref, y_ref):
    m = x_ref.shape[0] * x_ref.shape[1]
    two_d = lambda r: r[...].reshape(m, r.shape[-1])
    gated = (two_d(o_ref) * two_d(zs_ref)).astype(BF16)
    x1 = two_d(x_ref) + _dot(gated, wout_ref[...])
    y_ref[...] = _ple_tail(x1, two_d(p_ref), gw_ref, pw_ref).reshape(y_ref.shape)


def _fox_out(o, zs, x, p_all, layer, w_out, gate_w, ple_w, *, tm):
    bsz, s, d = x.shape
    pd = p_all.shape[-1]
    if s % tm == 0:
        nb, ns = 1, tm
    else:
        nb, ns = bsz, s
    tok = lambda width: pl.BlockSpec((nb, ns, width), lambda b, i: (b, i, 0))
    weights = (w_out, gate_w, ple_w)
    return pl.pallas_call(
        _fox_out_kernel,
        grid=(bsz // nb, s // ns),
        in_specs=[tok(o.shape[-1]), tok(zs.shape[-1]), tok(d),
                  pl.BlockSpec((None, nb, ns, pd), lambda b, i: (layer, b, i, 0))]
                 + [_const_spec(w.shape) for w in weights],
        out_specs=tok(d),
        out_shape=jax.ShapeDtypeStruct((bsz, s, d), F32),
        compiler_params=_params(("arbitrary", "arbitrary")),
        name="fox_out",
    )(o, zs, x, p_all, *weights)


TM_PROMPT = 256
TQ_PROMPT = 512
PAGES_PER_STEP = 8


def kernel(x_prompt, x_sample, state_conv, cache_k, cache_v, cache_logf, page_table, p_prompt, p_sample,
           a_norm, a_w_in, a_conv_w, a_conv_b, a_ln_g, a_ln_b, a_w_out, kv_norm, kv_w, kv_k_norm, kv_f_bias,
           b_norm, b_w_q, b_q_norm, b_w_out, ple_w, ple_gate_w):
    n_a = a_norm.shape[0]
    n_b = b_norm.shape[0]
    n_heads = kv_f_bias.shape[0]
    e_b = n_heads * HEAD_DIM
    d = x_prompt.shape[-1]
    row = lambda v: v.reshape(1, -1)
    bf = lambda w: w.astype(BF16)

    a_w_in16, a_w_out16, b_w_q16, b_w_out16 = bf(a_w_in), bf(a_w_out), bf(b_w_q), bf(b_w_out)
    ple_w16, gate_w16 = bf(ple_w), bf(ple_gate_w)
    w_kv16 = bf(kv_w[:, :2 * e_b])
    w_f16 = jnp.pad(bf(kv_w[:, 2 * e_b:]), ((0, 0), (0, LANES - n_heads)))
    f_bias = jnp.pad(row(kv_f_bias), ((0, 0), (0, LANES - n_heads)))
    k_gain = jnp.tile(row(kv_k_norm), (1, n_heads))
    bd = _block_diag_ones(n_heads)
    tm = min(TM_PROMPT, x_prompt.shape[1])
    tq = min(TQ_PROMPT, x_prompt.shape[1])

    def trunk(x, p_all, prev, prompt):
        conv_states = []
        for i in range(n_a):
            x, st = _conv_layer(x, p_all, i, None if prompt else prev[i], row(a_norm[i]), a_w_in16[i],
                                a_conv_w[i], row(a_conv_b[i]), row(a_ln_g[i]), row(a_ln_b[i]), a_w_out16[i],
                                gate_w16[i], ple_w16[i], tm=tm)
            conv_states.append(st)
        kv = _kv_proj(x, row(kv_norm), w_kv16, w_f16, f_bias, k_gain, bd, n_heads=n_heads, prompt=prompt, tm=tm)
        k_new, v_new, lf_new = kv[:3]
        for j in range(n_b):
            q_gain = jnp.tile(row(b_q_norm[j]), (1, n_heads)) * (LOG2E * HEAD_DIM ** -0.5)
            if prompt:
                qp, zs = _fox_q(x, kv[3], row(b_norm[j]), b_w_q16[j], q_gain, bd, n_heads=n_heads, prompt=True, tm=tm)
                o = _attn_prompt(qp, kv[4], kv[5], tq=tq)
            else:
                qn, zs = _fox_q(x, None, row(b_norm[j]), b_w_q16[j], q_gain, bd, n_heads=n_heads, prompt=False, tm=tm)
                o = _attn_decode(qn, k_new, v_new, lf_new, cache_k, cache_v, cache_logf, page_table,
                                 n_heads=n_heads, pages_per_step=min(PAGES_PER_STEP, page_table.shape[1]))
            x = _fox_out(o, zs, x, p_all, n_a + j, b_w_out16[j], gate_w16[n_a + j], ple_w16[n_a + j], tm=tm)
        shape4 = k_new.shape[:2] + (n_heads, HEAD_DIM)
        return x, jnp.stack(conv_states, axis=0), k_new.reshape(shape4), v_new.reshape(shape4), lf_new

    y_p, conv_p, k_p, v_p, lf_p = trunk(x_prompt, p_prompt, None, True)
    y_s, conv_s, k_s, v_s, lf_s = trunk(x_sample, p_sample, state_conv, False)
    return (y_p, y_s, conv_p, k_p, v_p, lf_p, conv_s, k_s, v_s, lf_s)
```

```python
import functools
import math

import numpy as np
import jax
import jax.numpy as jnp
from jax import lax
from jax.experimental import pallas as pl
from jax.experimental.pallas import tpu as pltpu

F32 = jnp.float32
BF16 = jnp.bfloat16

EPS = 1e-6
CONV_W = 31
HALO = CONV_W - 1
HEAD_DIM = 64
LANES = 128
SUBLANES = 8
HALO_PAD = 32
CONV_TIME_CHUNK = 8
CONV_LANE_CHUNK = 2 * LANES
LOG2E = math.log2(math.e)
NEG_BIG = -1e30
VMEM_LIMIT = 56 * 1024 * 1024


def _dot(a, b):
    return jnp.dot(a, b, preferred_element_type=F32)


def _dot_nt(a, b):
    return lax.dot_general(a, b, (((1,), (1,)), ((), ())), preferred_element_type=F32)


def _lane_tiles(w):
    *lead, k, n = w.shape
    return jnp.moveaxis(w.astype(BF16).reshape(*lead, k, n // LANES, LANES), -2, -3)


def _wdot(a, w_ref, col0=0, ncols=None, lead=()):
    first = col0 // LANES
    count = w_ref.shape[-3] - first if ncols is None else ncols // LANES
    tiles = [w_ref[lead + (j,)] for j in range(first, first + count)]
    return _dot(a, tiles[0] if count == 1 else jnp.concatenate(tiles, axis=1))


def _split3(x):
    p0 = x.astype(BF16)
    r = x - p0.astype(F32)
    p1 = r.astype(BF16)
    r = r - p1.astype(F32)
    return p0, p1, r.astype(BF16)


def _split2(x):
    p0 = x.astype(BF16)
    return p0, (x - p0.astype(F32)).astype(BF16)


def _rms(x, g):
    return x * lax.rsqrt(jnp.mean(x * x, axis=-1, keepdims=True) + EPS) * g


def _sigmoid(x):
    return 1.0 / (1.0 + jnp.exp(-x))


def _silu(x):
    return x * _sigmoid(x)


def _log_sigmoid(x):
    return jnp.minimum(x, 0.0) - jnp.log1p(jnp.exp(-jnp.abs(x)))


def _const_spec(shape):
    nd = len(shape)
    return pl.BlockSpec(shape, lambda *_: (0,) * nd, pipeline_mode=pl.Buffered(1))


def _params(sem):
    return pltpu.CompilerParams(dimension_semantics=sem, vmem_limit_bytes=VMEM_LIMIT)


def _ple_tail(x1, p, gw_ref, pw_ref):
    gate = _sigmoid(_wdot(x1.astype(BF16), gw_ref))
    return x1 + gate * _wdot(p.astype(BF16), pw_ref)


def _conv_layer_kernel(*refs, nb, t, time_chunk, lane_chunk, has_prev):
    if has_prev:
        (x_ref, p_ref, prev_ref, g_ref, win_ref, cw_ref, cb_ref, lng_ref, lnb_ref, wout_ref,
         gw_ref, pw_ref, y_ref, st_ref, xs_ref, ps_ref, ys_ref, uext_ref, c_ref, zs_ref) = refs
    else:
        (x_ref, p_ref, g_ref, win_ref, cw_ref, cb_ref, lng_ref, lnb_ref, wout_ref,
         gw_ref, pw_ref, y_ref, st_ref, xs_ref, ps_ref, ys_ref, uext_ref, c_ref, zs_ref) = refs
    e = cw_ref.shape[-1]

    for tt in range(t):
        xs_ref[pl.ds(tt * nb, nb), :] = x_ref[:, tt, :]
        ps_ref[pl.ds(tt * nb, nb), :] = p_ref[:, tt, :]

    if has_prev:
        uext_ref[pl.ds(HALO_PAD - HALO, HALO)] = prev_ref[...]
    else:
        s = pl.program_id(1)

        @pl.when(s == 0)
        def _():
            uext_ref[pl.ds(0, HALO_PAD)] = jnp.zeros((HALO_PAD, nb, e), F32)

        @pl.when(s > 0)
        def _():
            uext_ref[pl.ds(0, HALO_PAD)] = uext_ref[pl.ds(t, HALO_PAD)]

    h = _rms(xs_ref[...], g_ref[...]).astype(BF16)
    for l0 in range(0, e, lane_chunk):
        lanes = pl.ds(l0, lane_chunk)
        a = _wdot(h, win_ref, l0, lane_chunk)
        b = _wdot(h, win_ref, e + l0, lane_chunk)
        uext_ref[pl.ds(HALO_PAD, t), :, lanes] = (a * _sigmoid(b)).reshape(t, nb, lane_chunk)
        for t0 in range(0, t, time_chunk):
            for b0 in range(0, nb, SUBLANES):
                acc = jnp.broadcast_to(cb_ref[:, lanes], (time_chunk, SUBLANES, lane_chunk))
                for k in range(CONV_W):
                    win = uext_ref[pl.ds(HALO_PAD - HALO + t0 + k, time_chunk), pl.ds(b0, SUBLANES), lanes]
                    acc = acc + win * cw_ref[k, :, lanes]
                c_ref[pl.ds(t0, time_chunk), pl.ds(b0, SUBLANES), lanes] = acc
        zs_ref[:, lanes] = _silu(_wdot(h, win_ref, 2 * e + l0, lane_chunk))

    if has_prev:
        st_ref[...] = uext_ref[pl.ds(HALO_PAD + t - HALO, HALO)]
    else:
        @pl.when(pl.program_id(1) == pl.num_programs(1) - 1)
        def _():
            st_ref[...] = uext_ref[pl.ds(HALO_PAD + t - HALO, HALO)]

    th = t // 2
    for half in range(2):
        rows = pl.ds(half * th * nb, th * nb)
        c = c_ref[pl.ds(half * th, th)].reshape(th * nb, e)
        mu = jnp.mean(c, axis=-1, keepdims=True)
        cc = c - mu
        ln = cc * lax.rsqrt(jnp.mean(cc * cc, axis=-1, keepdims=True) + EPS) * lng_ref[...] + lnb_ref[...]
        gated = (_silu(ln) * zs_ref[rows, :]).astype(BF16)
        x1 = xs_ref[rows, :] + _wdot(gated, wout_ref)
        ys_ref[rows, :] = _ple_tail(x1, ps_ref[rows, :], gw_ref, pw_ref)
    for tt in range(t):
        y_ref[:, tt, :] = ys_ref[pl.ds(tt * nb, nb), :]


def _conv_layer(x, p_all, layer, prev, g, w_in, conv_w, conv_b, ln_g, ln_b, w_out, gate_w, ple_w, *, tm):
    bsz, s, d = x.shape
    e = conv_w.shape[1]
    pd = p_all.shape[-1]
    has_prev = prev is not None
    conv_w8 = jnp.broadcast_to(conv_w[:, None, :], (CONV_W, SUBLANES, e))
    conv_b8 = jnp.broadcast_to(conv_b, (SUBLANES, e))
    weights = (g, w_in, conv_w8, conv_b8, ln_g, ln_b, w_out, gate_w, ple_w)
    if has_prev:
        nb, t = bsz, s
    else:
        nb, t = SUBLANES, tm // SUBLANES
        assert s % t == 0
    time_chunk = min(t, CONV_TIME_CHUNK)
    assert bsz % nb == 0 and nb % SUBLANES == 0 and t % time_chunk == 0 and t % 2 == 0
    x_spec = pl.BlockSpec((nb, t, d), lambda gb, i: (gb, i, 0))
    st_spec = pl.BlockSpec((HALO, nb, e), lambda gb, i: (0, gb, 0))
    in_specs = [x_spec, pl.BlockSpec((None, nb, t, pd), lambda gb, i: (layer, gb, i, 0))]
    args = (x, p_all)
    if has_prev:
        in_specs.append(pl.BlockSpec((None, HALO, nb, e), lambda gb, i: (layer, 0, gb, 0)))
        args += (prev,)
    kern = functools.partial(_conv_layer_kernel, nb=nb, t=t, time_chunk=time_chunk,
                             lane_chunk=CONV_LANE_CHUNK, has_prev=has_prev)
    m = nb * t
    return pl.pallas_call(
        kern,
        grid=(bsz // nb, s // t),
        in_specs=in_specs + [_const_spec(w.shape) for w in weights],
        out_specs=[x_spec, st_spec],
        out_shape=[jax.ShapeDtypeStruct((bsz, s, d), F32), jax.ShapeDtypeStruct((HALO, bsz, e), F32)],
        scratch_shapes=[pltpu.VMEM((m, d), F32), pltpu.VMEM((m, pd), F32), pltpu.VMEM((m, d), F32),
                        pltpu.VMEM((HALO_PAD + t, nb, e), F32), pltpu.VMEM((t, nb, e), F32),
                        pltpu.VMEM((m, e), F32)],
        compiler_params=_params(("arbitrary", "arbitrary")),
        name="conv_layer_decode" if has_prev else "conv_layer_prompt",
    )(*args, *weights)


def _head_norm(raw, bd_ref, gain):
    hi, lo = _split2(raw * raw)
    ms = (_wdot(hi, bd_ref) + _wdot(lo, bd_ref)) * (1.0 / HEAD_DIM)
    return raw * lax.rsqrt(ms + EPS) * gain


def _head_tiles(arr, aug, n_heads, lane):
    for hd in range(n_heads):
        pair = arr[:, (hd // 2) * LANES:(hd // 2 + 1) * LANES]
        if hd % 2 == 1:
            pair = pltpu.roll(pair, HEAD_DIM, axis=1)
        yield hd, jnp.where(lane < HEAD_DIM, pair, aug[:, hd * LANES:(hd + 1) * LANES])


def _kv_kernel(*refs, n_heads, prompt):
    if prompt:
        (x_ref, g_ref, wkv_ref, wf_ref, fb_ref, kg_ref, bd_ref, sel_ref, ones_ref,
         k_ref, v_ref, lf_ref, c_ref, kp_ref, vp_ref, carry_ref) = refs
    else:
        (x_ref, g_ref, wkv_ref, wf_ref, fb_ref, kg_ref, bd_ref, k_ref, v_ref, lf_ref) = refs
    e_b = n_heads * HEAD_DIM
    m = x_ref.shape[-2]
    x = x_ref[...].reshape(m, x_ref.shape[-1])
    h = _rms(x, g_ref[...]).astype(BF16)
    kn = _head_norm(_wdot(h, wkv_ref, 0, e_b), bd_ref, kg_ref[...])
    v = _wdot(h, wkv_ref, e_b, e_b)
    logf = _log_sigmoid(_wdot(h, wf_ref) + fb_ref[...])
    k_ref[...] = kn.reshape(k_ref.shape)
    v_ref[...] = v.reshape(v_ref.shape)
    lf_ref[...] = logf[:, :n_heads].reshape(lf_ref.shape)
    if not prompt:
        return

    @pl.when(pl.program_id(1) == 0)
    def _():
        carry_ref[...] = jnp.zeros(carry_ref.shape, F32)

    row = lax.broadcasted_iota(jnp.int32, (m, m), 0)
    col = lax.broadcasted_iota(jnp.int32, (m, m), 1)
    tri = jnp.where(col <= row, 1.0, 0.0).astype(BF16)
    c = carry_ref[...]
    for piece in _split3(logf * LOG2E):
        c = c + _dot(tri, piece)
    carry_ref[...] = c[m - 1:m, :]
    c_ref[...] = c.reshape(c_ref.shape)

    aug = ones_ref[...]
    for i, piece in enumerate(_split3(-c)):
        aug = aug + _wdot(piece, sel_ref, lead=(i,))
    lane = lax.broadcasted_iota(jnp.int32, (m, LANES), 1)
    for hd, tile in _head_tiles(kn, aug, n_heads, lane):
        kp_ref[0, hd] = tile.astype(BF16)
    vaug = jnp.broadcast_to(jnp.where(lane == HEAD_DIM, 1.0, 0.0), (m, LANES))
    vaug = jnp.concatenate([vaug] * n_heads, axis=1)
    for hd, tile in _head_tiles(v, vaug, n_heads, lane):
        vp_ref[0, hd] = tile.astype(BF16)


def _sel_mats(n_heads, first_lane):
    sel = np.zeros((3, LANES, n_heads * LANES), np.float32)
    for i in range(3):
        for hd in range(n_heads):
            sel[i, hd, hd * LANES + first_lane + i] = 1.0
    return _lane_tiles(jnp.asarray(sel))


def _ones_row(n_heads, first_lane):
    row = np.zeros((1, n_heads * LANES), np.float32)
    for hd in range(n_heads):
        row[0, hd * LANES + first_lane:hd * LANES + first_lane + 3] = 1.0
    return jnp.asarray(row)


def _block_diag_ones(n_heads):
    idx = np.arange(n_heads * HEAD_DIM) // HEAD_DIM
    return _lane_tiles(jnp.asarray((idx[:, None] == idx[None, :]).astype(np.float32)))


def _kv_proj(x, g, w_kv, w_f, f_bias, k_gain, bd, *, n_heads, prompt, tm):
    bsz, s, d = x.shape
    e_b = n_heads * HEAD_DIM
    outs = [jax.ShapeDtypeStruct((bsz, s, e_b), F32), jax.ShapeDtypeStruct((bsz, s, e_b), F32),
            jax.ShapeDtypeStruct((bsz, s, n_heads), F32)]
    weights = [g, w_kv, w_f, f_bias, k_gain, bd]
    if prompt:
        assert s % tm == 0
        grid = (bsz, s // tm)
        weights += [_sel_mats(n_heads, HEAD_DIM), _ones_row(n_heads, HEAD_DIM + 3)]
        tok = lambda width: pl.BlockSpec((1, tm, width), lambda b, i: (b, i, 0))
        head_spec = pl.BlockSpec((1, n_heads, tm, LANES), lambda b, i: (b, 0, i, 0))
        out_specs = [tok(e_b), tok(e_b), tok(n_heads), tok(LANES), head_spec, head_spec]
        outs += [jax.ShapeDtypeStruct((bsz, s, LANES), F32),
                 jax.ShapeDtypeStruct((bsz, n_heads, s, LANES), BF16),
                 jax.ShapeDtypeStruct((bsz, n_heads, s, LANES), BF16)]
        x_spec = tok(d)
        scratch = [pltpu.VMEM((1, LANES), F32)]
        sem = ("arbitrary", "arbitrary")
    else:
        grid = (1,)
        x2 = x.reshape(1, bsz * s, d)
        full = lambda width: pl.BlockSpec((1, bsz * s, width), lambda i: (0, 0, 0))
        outs = [jax.ShapeDtypeStruct((1, bsz * s, o.shape[-1]), F32) for o in outs]
        out_specs = [full(e_b), full(e_b), full(n_heads)]
        x_spec = full(d)
        x = x2
        scratch = []
        sem = ("arbitrary",)
    res = pl.pallas_call(
        functools.partial(_kv_kernel, n_heads=n_heads, prompt=prompt),
        grid=grid,
        in_specs=[x_spec] + [_const_spec(w.shape) for w in weights],
        out_specs=out_specs,
        out_shape=outs,
        scratch_shapes=scratch,
        compiler_params=_params(sem),
        name="kv_proj_prompt" if prompt else "kv_proj_decode",
    )(x, *weights)
    if not prompt:
        res = [r.reshape(bsz, s, r.shape[-1]) for r in res]
    return res


def _fox_q_kernel(*refs, n_heads, prompt):
    if prompt:
        x_ref, c_ref, g_ref, wq_ref, qg_ref, bd_ref, sel_ref, ones_ref, qp_ref, zs_ref = refs
    else:
        x_ref, g_ref, wq_ref, qg_ref, bd_ref, qn_ref, zs_ref = refs
    e_b = n_heads * HEAD_DIM
    m = x_ref.shape[-2]
    x = x_ref[...].reshape(m, x_ref.shape[-1])
    h = _rms(x, g_ref[...]).astype(BF16)
    qn = _head_norm(_wdot(h, wq_ref, 0, e_b), bd_ref, qg_ref[...])
    zs_ref[...] = _silu(_wdot(h, wq_ref, e_b, e_b)).reshape(zs_ref.shape)
    if not prompt:
        qn_ref[...] = qn.reshape(qn_ref.shape)
        return
    aug = ones_ref[...]
    for i, piece in enumerate(_split3(c_ref[...].reshape(m, LANES))):
        aug = aug + _wdot(piece, sel_ref, lead=(i,))
    lane = lax.broadcasted_iota(jnp.int32, (m, LANES), 1)
    for hd, tile in _head_tiles(qn, aug, n_heads, lane):
        qp_ref[0, hd] = tile.astype(BF16)


def _fox_q(x, c, g, w_q, q_gain, bd, *, n_heads, prompt, tm):
    bsz, s, d = x.shape
    e_b = n_heads * HEAD_DIM
    if prompt:
        grid = (bsz, s // tm)
        tok = lambda width: pl.BlockSpec((1, tm, width), lambda b, i: (b, i, 0))
        weights = [g, w_q, q_gain, bd, _sel_mats(n_heads, HEAD_DIM + 3), _ones_row(n_heads, HEAD_DIM)]
        args = (x, c)
        in_specs = [tok(d), tok(LANES)]
        out_specs = [pl.BlockSpec((1, n_heads, tm, LANES), lambda b, i: (b, 0, i, 0)), tok(e_b)]
        outs = [jax.ShapeDtypeStruct((bsz, n_heads, s, LANES), BF16), jax.ShapeDtypeStruct((bsz, s, e_b), F32)]
        sem = ("arbitrary", "arbitrary")
    else:
        grid = (1,)
        full = lambda width: pl.BlockSpec((1, bsz * s, width), lambda i: (0, 0, 0))
        weights = [g, w_q, q_gain, bd]
        args = (x.reshape(1, bsz * s, d),)
        in_specs = [full(d)]
        out_specs = [full(e_b), full(e_b)]
        outs = [jax.ShapeDtypeStruct((1, bsz * s, e_b), F32)] * 2
        sem = ("arbitrary",)
    res = pl.pallas_call(
        functools.partial(_fox_q_kernel, n_heads=n_heads, prompt=prompt),
        grid=grid,
        in_specs=in_specs + [_const_spec(w.shape) for w in weights],
        out_specs=out_specs,
        out_shape=outs,
        compiler_params=_params(sem),
        name="fox_q_prompt" if prompt else "fox_q_decode",
    )(*args, *weights)
    if not prompt:
        res = [r.reshape(bsz, s, e_b) for r in res]
    return res


def _attn_prompt_kernel(q_ref, k_ref, v_ref, o_ref, *, tq, n_q):
    row = lax.broadcasted_iota(jnp.int32, (tq, tq), 0)
    col = lax.broadcasted_iota(jnp.int32, (tq, tq), 1)
    lane = lax.broadcasted_iota(jnp.int32, (tq, LANES), 1)
    n_h = q_ref.shape[1]

    def fold(j, hh, m_old, acc, masked):
        s = _dot_nt(q_ref[0, hh], k_ref[0, hh, pl.ds(j * tq, tq), :])
        if masked:
            s = jnp.where(col <= row, s, NEG_BIG)
        m_new = jnp.maximum(m_old, jnp.max(s, axis=-1, keepdims=True))
        p = jnp.exp2(s - m_new).astype(BF16)
        pv = _dot(p, v_ref[0, hh, pl.ds(j * tq, tq), :])
        return m_new, jnp.exp2(m_old - m_new) * acc + pv

    def attend(n_blocks):
        state = [(jnp.full((tq, 1), NEG_BIG, F32), jnp.zeros((tq, LANES), F32)) for _ in range(n_h)]
        for j in range(n_blocks):
            for hh in range(n_h):
                state[hh] = fold(j, hh, *state[hh], j == n_blocks - 1)
        outs = []
        for _, acc in state:
            denom = jnp.sum(jnp.where(lane == HEAD_DIM, acc, 0.0), axis=-1, keepdims=True)
            outs.append(acc / denom)
        for pair in range(n_h // 2):
            o_ref[0, :, pl.ds(pair * LANES, LANES)] = jnp.where(
                lane < HEAD_DIM, outs[2 * pair], pltpu.roll(outs[2 * pair + 1], HEAD_DIM, axis=1))

    for qi in range(n_q):
        pl.when(pl.program_id(2) == qi)(functools.partial(attend, qi + 1))


def _attn_prompt(qp, kp, vp, *, tq, heads_per_step):
    bsz, n_heads, s, _ = qp.shape
    hps = heads_per_step
    assert s % tq == 0 and n_heads % hps == 0 and hps % 2 == 0
    kv_spec = pl.BlockSpec((1, hps, s, LANES), lambda b, hp, i: (b, hp, 0, 0))
    return pl.pallas_call(
        functools.partial(_attn_prompt_kernel, tq=tq, n_q=s // tq),
        grid=(bsz, n_heads // hps, s // tq),
        in_specs=[pl.BlockSpec((1, hps, tq, LANES), lambda b, hp, i: (b, hp, i, 0)), kv_spec, kv_spec],
        out_specs=pl.BlockSpec((1, tq, hps * HEAD_DIM), lambda b, hp, i: (b, i, hp)),
        out_shape=jax.ShapeDtypeStruct((bsz, s, n_heads * HEAD_DIM), F32),
        compiler_params=_params(("arbitrary", "arbitrary", "arbitrary")),
        name="attn_prompt",
    )(qp, kp, vp)


def _attn_decode_kernel(*refs, n_pages_step, n_heads, t):
    g = n_pages_step
    qn_ref, kn_ref, vn_ref, lfn_ref, exp_ref, rep_ref = refs[1:7]
    k_refs = refs[7:7 + g]
    v_refs = refs[7 + g:7 + 2 * g]
    lf_refs = refs[7 + 2 * g:7 + 3 * g]
    o_ref = refs[7 + 3 * g]
    qexp_ref, m_ref, l_ref, acc_ref, carry_ref = refs[8 + 3 * g:]
    ht = n_heads * t
    e_b = n_heads * HEAD_DIM
    page = k_refs[0].shape[1]
    s_idx = pl.program_id(1)

    @pl.when(s_idx == 0)
    def _():
        spread = _dot(rep_ref[...], qn_ref[0].astype(BF16).astype(F32))
        keep = (lax.broadcasted_iota(jnp.int32, (ht, e_b), 0) // t
                == lax.broadcasted_iota(jnp.int32, (ht, e_b), 1) // HEAD_DIM)
        qexp_ref[...] = jnp.where(keep, spread, 0.0).astype(BF16)
        m_ref[...] = jnp.full(m_ref.shape, NEG_BIG, F32)
        l_ref[...] = jnp.zeros(l_ref.shape, F32)
        acc_ref[...] = jnp.zeros(acc_ref.shape, F32)
        carry_ref[...] = jnp.zeros(carry_ref.shape, F32)

    row = lax.broadcasted_iota(jnp.int32, (page, page), 0)
    col = lax.broadcasted_iota(jnp.int32, (page, page), 1)
    upper = jnp.where(row <= col, 1.0, 0.0).astype(BF16)

    def rows_per_head(a):
        return jnp.broadcast_to(a[:, None, :], (n_heads, t, a.shape[-1])).reshape(ht, a.shape[-1])

    def cumsum_keys(c, lf):
        for piece in _split3(lf):
            c = c + _dot(piece, upper)
        return c

    def page_bias(lf_t):
        c = cumsum_keys(jnp.broadcast_to(carry_ref[...], (n_heads, page)), lf_t * LOG2E)
        carry_ref[...] = c[:, page - 1:page]
        return c

    def update(s, pv):
        m_old = m_ref[...]
        m_new = jnp.maximum(m_old, jnp.max(s, axis=1, keepdims=True))
        p = jnp.exp2(s - m_new)
        alpha = jnp.exp2(m_old - m_new)
        m_ref[...] = m_new
        l_ref[...] = alpha * l_ref[...] + jnp.sum(p, axis=1, keepdims=True)
        acc_ref[...] = alpha * acc_ref[...] + pv(p.astype(BF16))

    qexp = qexp_ref[...]
    s = jnp.concatenate([_dot(qexp, k_refs[i][...].astype(BF16)) - rows_per_head(page_bias(lf_refs[i][...]))
                         for i in range(g)], axis=1)
    update(s, lambda p: sum(_dot_nt(p[:, i * page:(i + 1) * page], v_refs[i][...].astype(BF16))
                            for i in range(g)))

    @pl.when(s_idx == pl.num_programs(1) - 1)
    def _():
        pad = lambda a: jnp.concatenate([a, jnp.zeros((page - t, a.shape[1]), F32)], axis=0)
        lexp = jnp.zeros((page, ht), F32)
        for piece in _split3(pad(lfn_ref[0]) * LOG2E):
            lexp = lexp + _dot(piece, exp_ref[...])
        c = cumsum_keys(rows_per_head(jnp.broadcast_to(carry_ref[...], (n_heads, page))), lexp.T)
        s_new = _dot_nt(qexp, pad(kn_ref[0]).astype(BF16)) - c
        key = lax.broadcasted_iota(jnp.int32, (ht, page), 1)
        tok = lax.broadcasted_iota(jnp.int32, (ht, page), 0) % t
        v_pad = pad(vn_ref[0]).astype(BF16)
        update(jnp.where(key <= tok, s_new, NEG_BIG), lambda p: _dot(p, v_pad))
        keep = (lax.broadcasted_iota(jnp.int32, (ht, e_b), 0) // t
                == lax.broadcasted_iota(jnp.int32, (ht, e_b), 1) // HEAD_DIM)
        o = jnp.where(keep, acc_ref[...] / l_ref[...], 0.0)
        o_ref[0] = jnp.sum(o.reshape(n_heads, t, e_b), axis=0)


def _attn_decode(qn, k_new, v_new, lf_new, cache_k, cache_v, cache_logf, page_table, *, n_heads, pages_per_step):
    bsz, t, e_b = qn.shape
    n_pool, page, _, _ = cache_k.shape
    n_pages = page_table.shape[1]
    g = pages_per_step
    ht = n_heads * t
    assert n_pages % g == 0 and ht == page == LANES and t % SUBLANES == 0
    ck = jnp.transpose(cache_k, (0, 2, 3, 1)).reshape(n_pool, e_b, page)
    cv = jnp.transpose(cache_v, (0, 2, 3, 1)).reshape(n_pool, e_b, page)
    clf = jnp.transpose(cache_logf, (0, 2, 1))
    expand = jnp.asarray((np.arange(n_heads)[:, None] == (np.arange(ht)[None, :] // t)).astype(np.float32), BF16)
    repeat = jnp.asarray((np.arange(ht)[:, None] % t == np.arange(t)[None, :]).astype(np.float32))
    tok = lambda width: pl.BlockSpec((1, t, width), lambda b, s, pt: (b, 0, 0))
    page_spec = lambda rows, i: pl.BlockSpec((None, rows, page), lambda b, s, pt, i=i: (pt[b, s * g + i], 0, 0))
    in_specs = ([tok(e_b), tok(e_b), tok(e_b), tok(n_heads),
                 pl.BlockSpec(expand.shape, lambda b, s, pt: (0, 0)),
                 pl.BlockSpec(repeat.shape, lambda b, s, pt: (0, 0))]
                + [page_spec(e_b, i) for i in range(g)] * 2
                + [page_spec(n_heads, i) for i in range(g)])
    return pl.pallas_call(
        functools.partial(_attn_decode_kernel, n_pages_step=g, n_heads=n_heads, t=t),
        grid_spec=pltpu.PrefetchScalarGridSpec(
            num_scalar_prefetch=1,
            grid=(bsz, n_pages // g),
            in_specs=in_specs,
            out_specs=tok(e_b),
            scratch_shapes=[pltpu.VMEM((ht, e_b), BF16), pltpu.VMEM((ht, 1), F32), pltpu.VMEM((ht, 1), F32),
                            pltpu.VMEM((ht, e_b), F32), pltpu.VMEM((n_heads, 1), F32)]),
        out_shape=jax.ShapeDtypeStruct((bsz, t, e_b), F32),
        compiler_params=_params(("arbitrary", "arbitrary")),
        name="attn_decode",
    )(page_table, qn, k_new, v_new, lf_new, expand, repeat, *([ck] * g), *([cv] * g), *([clf] * g))


def _fox_out_kernel(o_ref, zs_ref, x_ref, p_ref, wout_ref, gw_ref, pw_ref, y_ref):
    m = x_ref.shape[0] * x_ref.shape[1]
    two_d = lambda r: r[...].reshape(m, r.shape[-1])
    gated = (two_d(o_ref) * two_d(zs_ref)).astype(BF16)
    x1 = two_d(x_ref) + _wdot(gated, wout_ref)
    y_ref[...] = _ple_tail(x1, two_d(p_ref), gw_ref, pw_ref).reshape(y_ref.shape)


def _fox_out(o, zs, x, p_all, layer, w_out, gate_w, ple_w, *, tm):
    bsz, s, d = x.shape
    pd = p_all.shape[-1]
    if s % tm == 0:
        nb, ns = 1, tm
    else:
        nb, ns = bsz, s
    tok = lambda width: pl.BlockSpec((nb, ns, width), lambda b, i: (b, i, 0))
    weights = (w_out, gate_w, ple_w)
    return pl.pallas_call(
        _fox_out_kernel,
        grid=(bsz // nb, s // ns),
        in_specs=[tok(o.shape[-1]), tok(zs.shape[-1]), tok(d),
                  pl.BlockSpec((None, nb, ns, pd), lambda b, i: (layer, b, i, 0))]
                 + [_const_spec(w.shape) for w in weights],
        out_specs=tok(d),
        out_shape=jax.ShapeDtypeStruct((bsz, s, d), F32),
        compiler_params=_params(("arbitrary", "arbitrary")),
        name="fox_out",
    )(o, zs, x, p_all, *weights)


TM_PROMPT = 256
TQ_PROMPT = 512
ATTN_HEADS_PER_STEP = 2
PAGES_PER_STEP = 8


def kernel(x_prompt, x_sample, state_conv, cache_k, cache_v, cache_logf, page_table, p_prompt, p_sample,
           a_norm, a_w_in, a_conv_w, a_conv_b, a_ln_g, a_ln_b, a_w_out, kv_norm, kv_w, kv_k_norm, kv_f_bias,
           b_norm, b_w_q, b_q_norm, b_w_out, ple_w, ple_gate_w):
    n_a = a_norm.shape[0]
    n_b = b_norm.shape[0]
    n_heads = kv_f_bias.shape[0]
    e_b = n_heads * HEAD_DIM
    d = x_prompt.shape[-1]
    row = lambda v: v.reshape(1, -1)
    bf = _lane_tiles

    a_w_in16, a_w_out16, b_w_q16, b_w_out16 = bf(a_w_in), bf(a_w_out), bf(b_w_q), bf(b_w_out)
    ple_w16, gate_w16 = bf(ple_w), bf(ple_gate_w)
    w_kv16 = bf(kv_w[:, :2 * e_b])
    w_f16 = bf(jnp.pad(kv_w[:, 2 * e_b:], ((0, 0), (0, LANES - n_heads))))
    f_bias = jnp.pad(row(kv_f_bias), ((0, 0), (0, LANES - n_heads)))
    k_gain = jnp.tile(row(kv_k_norm), (1, n_heads))
    bd = _block_diag_ones(n_heads)
    tm = min(TM_PROMPT, x_prompt.shape[1])
    tq = min(TQ_PROMPT, x_prompt.shape[1])

    def trunk(x, p_all, prev, prompt):
        conv_states = []
        if prev is not None:
            prev = jnp.transpose(prev, (0, 2, 1, 3))
        for i in range(n_a):
            x, st = _conv_layer(x, p_all, i, prev, row(a_norm[i]), a_w_in16[i],
                                a_conv_w[i], row(a_conv_b[i]), row(a_ln_g[i]), row(a_ln_b[i]), a_w_out16[i],
                                gate_w16[i], ple_w16[i], tm=tm)
            conv_states.append(st)
        kv = _kv_proj(x, row(kv_norm), w_kv16, w_f16, f_bias, k_gain, bd, n_heads=n_heads, prompt=prompt, tm=tm)
        k_new, v_new, lf_new = kv[:3]
        for j in range(n_b):
            q_gain = jnp.tile(row(b_q_norm[j]), (1, n_heads)) * (LOG2E * HEAD_DIM ** -0.5)
            if prompt:
                qp, zs = _fox_q(x, kv[3], row(b_norm[j]), b_w_q16[j], q_gain, bd, n_heads=n_heads, prompt=True, tm=tm)
                o = _attn_prompt(qp, kv[4], kv[5], tq=tq, heads_per_step=ATTN_HEADS_PER_STEP)
            else:
                qn, zs = _fox_q(x, None, row(b_norm[j]), b_w_q16[j], q_gain, bd, n_heads=n_heads, prompt=False, tm=tm)
                o = _attn_decode(qn, k_new, v_new, lf_new, cache_k, cache_v, cache_logf, page_table,
                                 n_heads=n_heads, pages_per_step=min(PAGES_PER_STEP, page_table.shape[1]))
            x = _fox_out(o, zs, x, p_all, n_a + j, b_w_out16[j], gate_w16[n_a + j], ple_w16[n_a + j], tm=tm)
        shape4 = k_new.shape[:2] + (n_heads, HEAD_DIM)
        conv_state = jnp.transpose(jnp.stack(conv_states, axis=0), (0, 2, 1, 3))
        return x, conv_state, k_new.reshape(shape4), v_new.reshape(shape4), lf_new

    y_p, conv_p, k_p, v_p, lf_p = trunk(x_prompt, p_prompt, None, True)
    y_s, conv_s, k_s, v_s, lf_s = trunk(x_sample, p_sample, state_conv, False)
    return (y_p, y_s, conv_p, k_p, v_p, lf_p, conv_s, k_s, v_s, lf_s)
```

```python
import functools
import math

import numpy as np
import jax
import jax.numpy as jnp
from jax import lax
from jax.experimental import pallas as pl
from jax.experimental.pallas import tpu as pltpu

F32 = jnp.float32
BF16 = jnp.bfloat16

EPS = 1e-6
CONV_W = 31
HALO = CONV_W - 1
HEAD_DIM = 64
LANES = 128
SUBLANES = 8
MXU_DEPTH = 256
HALO_PAD = 32
CONV_TIME_CHUNK = 8
CONV_LANE_CHUNK = 2 * LANES
LOG2E = math.log2(math.e)
NEG_BIG = -1e30
VMEM_LIMIT = 56 * 1024 * 1024


def _dot(a, b):
    return jnp.dot(a, b, preferred_element_type=F32)


def _dot_nt(a, b):
    return lax.dot_general(a, b, (((1,), (1,)), ((), ())), preferred_element_type=F32)


def _lane_tiles(w):
    *lead, k, n = w.shape
    return jnp.moveaxis(w.astype(BF16).reshape(*lead, k, n // LANES, LANES), -2, -3)


def _wdot(a, w_ref, col0=0, ncols=None, lead=()):
    first = col0 // LANES
    count = w_ref.shape[-3] - first if ncols is None else ncols // LANES
    tiles = [w_ref[lead + (j,)] for j in range(first, first + count)]
    return _dot(a, tiles[0] if count == 1 else jnp.concatenate(tiles, axis=1))


def _split3(x):
    p0 = x.astype(BF16)
    r = x - p0.astype(F32)
    p1 = r.astype(BF16)
    r = r - p1.astype(F32)
    return p0, p1, r.astype(BF16)


def _split2(x):
    p0 = x.astype(BF16)
    return p0, (x - p0.astype(F32)).astype(BF16)


def _rms(x, g):
    return x * lax.rsqrt(jnp.mean(x * x, axis=-1, keepdims=True) + EPS) * g


def _sigmoid(x):
    return 1.0 / (1.0 + jnp.exp(-x))


def _silu(x):
    return x * _sigmoid(x)


def _log_sigmoid(x):
    return jnp.minimum(x, 0.0) - jnp.log1p(jnp.exp(-jnp.abs(x)))


def _const_spec(shape):
    nd = len(shape)
    return pl.BlockSpec(shape, lambda *_: (0,) * nd, pipeline_mode=pl.Buffered(1))


def _params(sem):
    return pltpu.CompilerParams(dimension_semantics=sem, vmem_limit_bytes=VMEM_LIMIT)


def _ple_tail(x1, p, gw_ref, pw_ref):
    gate = _sigmoid(_wdot(x1.astype(BF16), gw_ref))
    return x1 + gate * _wdot(p.astype(BF16), pw_ref)


def _conv_layer_kernel(*refs, nb, t, time_chunk, lane_chunk, has_prev):
    if has_prev:
        (x_ref, p_ref, prev_ref, g_ref, win_ref, cw_ref, cb_ref, lng_ref, lnb_ref, wout_ref,
         gw_ref, pw_ref, y_ref, st_ref, xs_ref, ps_ref, ys_ref, uext_ref, c_ref, zs_ref) = refs
    else:
        (x_ref, p_ref, g_ref, win_ref, cw_ref, cb_ref, lng_ref, lnb_ref, wout_ref,
         gw_ref, pw_ref, y_ref, st_ref, xs_ref, ps_ref, ys_ref, uext_ref, c_ref, zs_ref) = refs
    e = cw_ref.shape[-1]

    for tt in range(t):
        xs_ref[pl.ds(tt * nb, nb), :] = x_ref[:, tt, :]
        ps_ref[pl.ds(tt * nb, nb), :] = p_ref[:, tt, :]

    if has_prev:
        uext_ref[pl.ds(HALO_PAD - HALO, HALO)] = prev_ref[...]
    else:
        s = pl.program_id(1)

        @pl.when(s == 0)
        def _():
            uext_ref[pl.ds(0, HALO_PAD)] = jnp.zeros((HALO_PAD, nb, e), F32)

        @pl.when(s > 0)
        def _():
            uext_ref[pl.ds(0, HALO_PAD)] = uext_ref[pl.ds(t, HALO_PAD)]

    h = _rms(xs_ref[...], g_ref[...]).astype(BF16)
    for l0 in range(0, e, lane_chunk):
        lanes = pl.ds(l0, lane_chunk)
        a = _wdot(h, win_ref, l0, lane_chunk)
        b = _wdot(h, win_ref, e + l0, lane_chunk)
        uext_ref[pl.ds(HALO_PAD, t), :, lanes] = (a * _sigmoid(b)).reshape(t, nb, lane_chunk)
        for t0 in range(0, t, time_chunk):
            for b0 in range(0, nb, SUBLANES):
                acc = jnp.broadcast_to(cb_ref[:, lanes], (time_chunk, SUBLANES, lane_chunk))
                for k in range(CONV_W):
                    win = uext_ref[pl.ds(HALO_PAD - HALO + t0 + k, time_chunk), pl.ds(b0, SUBLANES), lanes]
                    acc = acc + win * cw_ref[k, :, lanes]
                c_ref[pl.ds(t0, time_chunk), pl.ds(b0, SUBLANES), lanes] = acc
        zs_ref[:, lanes] = _silu(_wdot(h, win_ref, 2 * e + l0, lane_chunk))

    if has_prev:
        st_ref[...] = uext_ref[pl.ds(HALO_PAD + t - HALO, HALO)]
    else:
        @pl.when(pl.program_id(1) == pl.num_programs(1) - 1)
        def _():
            st_ref[...] = uext_ref[pl.ds(HALO_PAD + t - HALO, HALO)]

    th = t // 2
    for half in range(2):
        rows = pl.ds(half * th * nb, th * nb)
        c = c_ref[pl.ds(half * th, th)].reshape(th * nb, e)
        mu = jnp.mean(c, axis=-1, keepdims=True)
        cc = c - mu
        ln = cc * lax.rsqrt(jnp.mean(cc * cc, axis=-1, keepdims=True) + EPS) * lng_ref[...] + lnb_ref[...]
        gated = (_silu(ln) * zs_ref[rows, :]).astype(BF16)
        x1 = xs_ref[rows, :] + _wdot(gated, wout_ref)
        ys_ref[rows, :] = _ple_tail(x1, ps_ref[rows, :], gw_ref, pw_ref)
    for tt in range(t):
        y_ref[:, tt, :] = ys_ref[pl.ds(tt * nb, nb), :]


def _conv_layer(x, p_all, layer, prev, g, w_in, conv_w, conv_b, ln_g, ln_b, w_out, gate_w, ple_w, *, tm):
    bsz, s, d = x.shape
    e = conv_w.shape[1]
    pd = p_all.shape[-1]
    has_prev = prev is not None
    conv_w8 = jnp.broadcast_to(conv_w[:, None, :], (CONV_W, SUBLANES, e))
    conv_b8 = jnp.broadcast_to(conv_b, (SUBLANES, e))
    weights = (g, w_in, conv_w8, conv_b8, ln_g, ln_b, w_out, gate_w, ple_w)
    if has_prev:
        nb, t = bsz, s
    else:
        nb, t = SUBLANES, tm // SUBLANES
        assert s % t == 0
    time_chunk = min(t, CONV_TIME_CHUNK)
    assert bsz % nb == 0 and nb % SUBLANES == 0 and t % time_chunk == 0 and t % 2 == 0
    x_spec = pl.BlockSpec((nb, t, d), lambda gb, i: (gb, i, 0))
    st_spec = pl.BlockSpec((HALO, nb, e), lambda gb, i: (0, gb, 0))
    in_specs = [x_spec, pl.BlockSpec((None, nb, t, pd), lambda gb, i: (layer, gb, i, 0))]
    args = (x, p_all)
    if has_prev:
        in_specs.append(pl.BlockSpec((None, HALO, nb, e), lambda gb, i: (layer, 0, gb, 0)))
        args += (prev,)
    kern = functools.partial(_conv_layer_kernel, nb=nb, t=t, time_chunk=time_chunk,
                             lane_chunk=CONV_LANE_CHUNK, has_prev=has_prev)
    m = nb * t
    return pl.pallas_call(
        kern,
        grid=(bsz // nb, s // t),
        in_specs=in_specs + [_const_spec(w.shape) for w in weights],
        out_specs=[x_spec, st_spec],
        out_shape=[jax.ShapeDtypeStruct((bsz, s, d), F32), jax.ShapeDtypeStruct((HALO, bsz, e), F32)],
        scratch_shapes=[pltpu.VMEM((m, d), F32), pltpu.VMEM((m, pd), F32), pltpu.VMEM((m, d), F32),
                        pltpu.VMEM((HALO_PAD + t, nb, e), F32), pltpu.VMEM((t, nb, e), F32),
                        pltpu.VMEM((m, e), F32)],
        compiler_params=_params(("arbitrary", "arbitrary")),
        name="conv_layer_decode" if has_prev else "conv_layer_prompt",
    )(*args, *weights)


def _head_norm(raw, bd_ref, gain):
    ms = _wdot((raw * raw).astype(BF16), bd_ref) * (1.0 / HEAD_DIM)
    return raw * lax.rsqrt(ms + EPS) * gain


def _head_tiles(arr, n_heads, lane, extra):
    for hd in range(n_heads):
        pair = arr[:, (hd // 2) * LANES:(hd // 2 + 1) * LANES]
        if hd % 2 == 1:
            pair = pltpu.roll(pair, HEAD_DIM, axis=1)
        yield hd, jnp.where(lane < HEAD_DIM, pair, extra(hd))


def _bias_lanes(c, lane, piece_lane, ones_lane):
    pieces = [p.astype(F32) for p in _split3(c)]
    base = jnp.where((lane >= ones_lane) & (lane < ones_lane + 3), 1.0, 0.0)

    def extra(hd):
        tile = base
        for i, piece in enumerate(pieces):
            tile = jnp.where(lane == piece_lane + i, pltpu.roll(piece, (piece_lane + i - hd) % LANES, axis=1), tile)
        return tile

    return extra


def _kv_kernel(*refs, n_heads, prompt):
    if prompt:
        (x_ref, g_ref, wkv_ref, wf_ref, fb_ref, kg_ref, bd_ref,
         k_ref, v_ref, lf_ref, c_ref, kp_ref, vp_ref, carry_ref) = refs
    else:
        (x_ref, g_ref, wkv_ref, wf_ref, fb_ref, kg_ref, bd_ref, k_ref, v_ref, lf_ref) = refs
    e_b = n_heads * HEAD_DIM
    m = x_ref.shape[-2]
    x = x_ref[...].reshape(m, x_ref.shape[-1])
    h = _rms(x, g_ref[...]).astype(BF16)
    kn = _head_norm(_wdot(h, wkv_ref, 0, e_b), bd_ref, kg_ref[...])
    v = _wdot(h, wkv_ref, e_b, e_b)
    logf = _log_sigmoid(_wdot(h, wf_ref) + fb_ref[...])
    k_ref[...] = kn.reshape(k_ref.shape)
    v_ref[...] = v.reshape(v_ref.shape)
    lf_ref[...] = logf[:, :n_heads].reshape(lf_ref.shape)
    if not prompt:
        return

    @pl.when(pl.program_id(1) == 0)
    def _():
        carry_ref[...] = jnp.zeros(carry_ref.shape, F32)

    row = lax.broadcasted_iota(jnp.int32, (m, m), 0)
    col = lax.broadcasted_iota(jnp.int32, (m, m), 1)
    tri = jnp.where(col <= row, 1.0, 0.0).astype(BF16)
    c = carry_ref[...]
    for piece in _split3(logf * LOG2E):
        c = c + _dot(tri, piece)
    carry_ref[...] = c[m - 1:m, :]
    c_ref[...] = c.reshape(c_ref.shape)

    lane = lax.broadcasted_iota(jnp.int32, (m, LANES), 1)
    for hd, tile in _head_tiles(kn, n_heads, lane, _bias_lanes(-c, lane, HEAD_DIM, HEAD_DIM + 3)):
        kp_ref[0, hd] = tile.astype(BF16)
    ones_col = jnp.where(lane == HEAD_DIM, 1.0, 0.0)
    for hd, tile in _head_tiles(v, n_heads, lane, lambda hd: ones_col):
        vp_ref[0, hd] = tile.astype(BF16)


def _block_diag_ones(n_heads):
    idx = np.arange(n_heads * HEAD_DIM) // HEAD_DIM
    return _lane_tiles(jnp.asarray((idx[:, None] == idx[None, :]).astype(np.float32)))


def _kv_proj(x, g, w_kv, w_f, f_bias, k_gain, bd, *, n_heads, prompt, tm):
    bsz, s, d = x.shape
    e_b = n_heads * HEAD_DIM
    outs = [jax.ShapeDtypeStruct((bsz, s, e_b), F32), jax.ShapeDtypeStruct((bsz, s, e_b), F32),
            jax.ShapeDtypeStruct((bsz, s, n_heads), F32)]
    weights = [g, w_kv, w_f, f_bias, k_gain, bd]
    if prompt:
        assert s % tm == 0
        grid = (bsz, s // tm)
        tok = lambda width: pl.BlockSpec((1, tm, width), lambda b, i: (b, i, 0))
        head_spec = pl.BlockSpec((1, n_heads, tm, LANES), lambda b, i: (b, 0, i, 0))
        out_specs = [tok(e_b), tok(e_b), tok(n_heads), tok(LANES), head_spec, head_spec]
        outs += [jax.ShapeDtypeStruct((bsz, s, LANES), F32),
                 jax.ShapeDtypeStruct((bsz, n_heads, s, LANES), BF16),
                 jax.ShapeDtypeStruct((bsz, n_heads, s, LANES), BF16)]
        x_spec = tok(d)
        scratch = [pltpu.VMEM((1, LANES), F32)]
        sem = ("arbitrary", "arbitrary")
    else:
        grid = (1,)
        x2 = x.reshape(1, bsz * s, d)
        full = lambda width: pl.BlockSpec((1, bsz * s, width), lambda i: (0, 0, 0))
        outs = [jax.ShapeDtypeStruct((1, bsz * s, o.shape[-1]), F32) for o in outs]
        out_specs = [full(e_b), full(e_b), full(n_heads)]
        x_spec = full(d)
        x = x2
        scratch = []
        sem = ("arbitrary",)
    res = pl.pallas_call(
        functools.partial(_kv_kernel, n_heads=n_heads, prompt=prompt),
        grid=grid,
        in_specs=[x_spec] + [_const_spec(w.shape) for w in weights],
        out_specs=out_specs,
        out_shape=outs,
        scratch_shapes=scratch,
        compiler_params=_params(sem),
        name="kv_proj_prompt" if prompt else "kv_proj_decode",
    )(x, *weights)
    if not prompt:
        res = [r.reshape(bsz, s, r.shape[-1]) for r in res]
    return res


def _fox_q_kernel(*refs, n_heads, prompt):
    if prompt:
        x_ref, c_ref, g_ref, wq_ref, qg_ref, bd_ref, qp_ref, zs_ref = refs
    else:
        x_ref, g_ref, wq_ref, qg_ref, bd_ref, qn_ref, zs_ref = refs
    e_b = n_heads * HEAD_DIM
    m = x_ref.shape[-2]
    x = x_ref[...].reshape(m, x_ref.shape[-1])
    h = _rms(x, g_ref[...]).astype(BF16)
    qn = _head_norm(_wdot(h, wq_ref, 0, e_b), bd_ref, qg_ref[...])
    zs_ref[...] = _silu(_wdot(h, wq_ref, e_b, e_b)).reshape(zs_ref.shape)
    if not prompt:
        qn_ref[...] = qn.reshape(qn_ref.shape)
        return
    lane = lax.broadcasted_iota(jnp.int32, (m, LANES), 1)
    extra = _bias_lanes(c_ref[...].reshape(m, LANES), lane, HEAD_DIM + 3, HEAD_DIM)
    for hd, tile in _head_tiles(qn, n_heads, lane, extra):
        qp_ref[0, hd] = tile.astype(BF16)


def _fox_q(x, c, g, w_q, q_gain, bd, *, n_heads, prompt, tm):
    bsz, s, d = x.shape
    e_b = n_heads * HEAD_DIM
    if prompt:
        grid = (bsz, s // tm)
        tok = lambda width: pl.BlockSpec((1, tm, width), lambda b, i: (b, i, 0))
        weights = [g, w_q, q_gain, bd]
        args = (x, c)
        in_specs = [tok(d), tok(LANES)]
        out_specs = [pl.BlockSpec((1, n_heads, tm, LANES), lambda b, i: (b, 0, i, 0)), tok(e_b)]
        outs = [jax.ShapeDtypeStruct((bsz, n_heads, s, LANES), BF16), jax.ShapeDtypeStruct((bsz, s, e_b), F32)]
        sem = ("arbitrary", "arbitrary")
    else:
        grid = (1,)
        full = lambda width: pl.BlockSpec((1, bsz * s, width), lambda i: (0, 0, 0))
        weights = [g, w_q, q_gain, bd]
        args = (x.reshape(1, bsz * s, d),)
        in_specs = [full(d)]
        out_specs = [full(e_b), full(e_b)]
        outs = [jax.ShapeDtypeStruct((1, bsz * s, e_b), F32)] * 2
        sem = ("arbitrary",)
    res = pl.pallas_call(
        functools.partial(_fox_q_kernel, n_heads=n_heads, prompt=prompt),
        grid=grid,
        in_specs=in_specs + [_const_spec(w.shape) for w in weights],
        out_specs=out_specs,
        out_shape=outs,
        compiler_params=_params(sem),
        name="fox_q_prompt" if prompt else "fox_q_decode",
    )(*args, *weights)
    if not prompt:
        res = [r.reshape(bsz, s, e_b) for r in res]
    return res


def _attn_prompt_kernel(q_ref, k_ref, v_ref, o_ref, *, tq, n_q):
    row = lax.broadcasted_iota(jnp.int32, (tq, tq), 0)
    col = lax.broadcasted_iota(jnp.int32, (tq, tq), 1)
    lane = lax.broadcasted_iota(jnp.int32, (tq, LANES), 1)
    n_h = q_ref.shape[1]

    def fold(j, hh, m_old, acc, masked):
        s = _dot_nt(q_ref[0, hh], k_ref[0, hh, pl.ds(j * tq, tq), :])
        if masked:
            s = jnp.where(col <= row, s, NEG_BIG)
        m_new = jnp.maximum(m_old, jnp.max(s, axis=-1, keepdims=True))
        p = jnp.exp2(s - m_new).astype(BF16)
        pv = _dot(p, v_ref[0, hh, pl.ds(j * tq, tq), :])
        return m_new, jnp.exp2(m_old - m_new) * acc + pv

    def attend(n_blocks):
        state = [(jnp.full((tq, 1), NEG_BIG, F32), jnp.zeros((tq, LANES), F32)) for _ in range(n_h)]
        for j in range(n_blocks):
            for hh in range(n_h):
                state[hh] = fold(j, hh, *state[hh], j == n_blocks - 1)
        outs = []
        for _, acc in state:
            denom = jnp.sum(jnp.where(lane == HEAD_DIM, acc, 0.0), axis=-1, keepdims=True)
            outs.append(acc / denom)
        for pair in range(n_h // 2):
            o_ref[0, :, pl.ds(pair * LANES, LANES)] = jnp.where(
                lane < HEAD_DIM, outs[2 * pair], pltpu.roll(outs[2 * pair + 1], HEAD_DIM, axis=1))

    for qi in range(n_q):
        pl.when(pl.program_id(2) == qi)(functools.partial(attend, qi + 1))


def _attn_prompt(qp, kp, vp, *, tq, heads_per_step):
    bsz, n_heads, s, _ = qp.shape
    hps = heads_per_step
    assert s % tq == 0 and n_heads % hps == 0 and hps % 2 == 0
    kv_spec = pl.BlockSpec((1, hps, s, LANES), lambda b, hp, i: (b, hp, 0, 0))
    return pl.pallas_call(
        functools.partial(_attn_prompt_kernel, tq=tq, n_q=s // tq),
        grid=(bsz, n_heads // hps, s // tq),
        in_specs=[pl.BlockSpec((1, hps, tq, LANES), lambda b, hp, i: (b, hp, i, 0)), kv_spec, kv_spec],
        out_specs=pl.BlockSpec((1, tq, hps * HEAD_DIM), lambda b, hp, i: (b, i, hp)),
        out_shape=jax.ShapeDtypeStruct((bsz, s, n_heads * HEAD_DIM), F32),
        compiler_params=_params(("arbitrary", "arbitrary", "arbitrary")),
        name="attn_prompt",
    )(qp, kp, vp)


def _attn_decode_kernel(*refs, n_pages_step, n_heads, t, group_heads):
    g = n_pages_step
    qn_ref, kn_ref, vn_ref, lfn_ref, exp_ref, rep_ref = refs[1:7]
    k_refs = refs[7:7 + g]
    v_refs = refs[7 + g:7 + 2 * g]
    lf_refs = refs[7 + 2 * g:7 + 3 * g]
    o_ref = refs[7 + 3 * g]
    qexp_ref, m_ref, l_ref, acc_ref, carry_ref = refs[8 + 3 * g:]
    ht = n_heads * t
    e_b = n_heads * HEAD_DIM
    page = k_refs[0].shape[1]
    s_idx = pl.program_id(1)

    @pl.when(s_idx == 0)
    def _():
        spread = _dot(rep_ref[...], qn_ref[0].astype(BF16).astype(F32))
        keep = (lax.broadcasted_iota(jnp.int32, (ht, e_b), 0) // t
                == lax.broadcasted_iota(jnp.int32, (ht, e_b), 1) // HEAD_DIM)
        qexp_ref[...] = jnp.where(keep, spread, 0.0).astype(BF16)
        m_ref[...] = jnp.full(m_ref.shape, NEG_BIG, F32)
        l_ref[...] = jnp.zeros(l_ref.shape, F32)
        acc_ref[...] = jnp.zeros(acc_ref.shape, F32)
        carry_ref[...] = jnp.zeros(carry_ref.shape, F32)

    row = lax.broadcasted_iota(jnp.int32, (page, page), 0)
    col = lax.broadcasted_iota(jnp.int32, (page, page), 1)
    upper = jnp.where(row <= col, 1.0, 0.0).astype(BF16)

    def rows_per_head(a):
        return jnp.broadcast_to(a[:, None, :], (n_heads, t, a.shape[-1])).reshape(ht, a.shape[-1])

    def cumsum_keys(c, lf):
        for piece in _split3(lf):
            c = c + _dot(piece, upper)
        return c

    def page_bias(lf_t):
        lf2 = lf_t * LOG2E
        before = carry_ref[...]
        carry_ref[...] = before + jnp.sum(lf2, axis=1, keepdims=True)
        return cumsum_keys(jnp.broadcast_to(before, (n_heads, page)), lf2)

    gr, gc = group_heads * t, group_heads * HEAD_DIM
    n_groups = n_heads // group_heads
    rows = lambda a, c: a[c * gr:(c + 1) * gr]
    q_group = lambda c: qexp_ref[pl.ds(c * gr, gr), pl.ds(c * gc, gc)]
    slab = lambda ref, c: ref[pl.ds(c * gc, gc), :].astype(BF16)
    by_group = lambda f: jnp.concatenate([f(c) for c in range(n_groups)], axis=0)

    def update(s, pv_t):
        m_old = m_ref[...]
        m_new = jnp.maximum(m_old, jnp.max(s, axis=1, keepdims=True))
        p = jnp.exp2(s - m_new)
        alpha = jnp.exp2(m_old - m_new)
        m_ref[...] = m_new
        l_ref[...] = alpha * l_ref[...] + jnp.sum(p, axis=1, keepdims=True)
        alpha_row = jnp.broadcast_to(alpha, (ht, ht)).T[0:1, :]
        p = p.astype(BF16)
        for c in range(n_groups):
            acc_ref[c, :, pl.ds(0, gr)] = (alpha_row[:, c * gr:(c + 1) * gr] * acc_ref[c, :, pl.ds(0, gr)]
                                           + pv_t(p, c))

    s = jnp.concatenate(
        [by_group(lambda c: _dot(q_group(c), slab(k_refs[i], c))) - rows_per_head(page_bias(lf_refs[i][...]))
         for i in range(g)], axis=1)
    update(s, lambda p, c: sum(
        _dot_nt(jnp.concatenate([slab(v_refs[i], c), slab(v_refs[i + 1], c)], axis=1),
                rows(p, c)[:, i * page:(i + 2) * page]) for i in range(0, g, 2)))

    @pl.when(s_idx == pl.num_programs(1) - 1)
    def _():
        pad = lambda a: jnp.concatenate([a, jnp.zeros((page - t, a.shape[1]), F32)], axis=0)
        lexp = jnp.zeros((page, ht), F32)
        for piece in _split3(pad(lfn_ref[0]) * LOG2E):
            lexp = lexp + _dot(piece, exp_ref[...])
        c_new = cumsum_keys(rows_per_head(jnp.broadcast_to(carry_ref[...], (n_heads, page))), lexp.T)
        k_pad = pad(kn_ref[0]).astype(BF16)
        v_pad = pad(vn_ref[0]).astype(BF16)
        s_new = by_group(lambda c: _dot_nt(q_group(c), k_pad[:, c * gc:(c + 1) * gc])) - c_new
        key = lax.broadcasted_iota(jnp.int32, (ht, page), 1)
        tok = lax.broadcasted_iota(jnp.int32, (ht, page), 0) % t
        tall = lambda a: jnp.concatenate([a, jnp.zeros((ht - a.shape[0], a.shape[1]), F32)], axis=0)
        update(jnp.where(key <= tok, s_new, NEG_BIG),
               lambda p, c: tall(_dot(rows(p, c), v_pad[:, c * gc:(c + 1) * gc])).T[:, :gr])
        keep = (lax.broadcasted_iota(jnp.int32, (gr, gc), 0) // t
                == lax.broadcasted_iota(jnp.int32, (gr, gc), 1) // HEAD_DIM)
        outs = []
        for c in range(n_groups):
            o = acc_ref[c].T[:gr, :] / rows(l_ref[...], c)
            outs.append(jnp.sum(jnp.where(keep, o, 0.0).reshape(group_heads, t, gc), axis=0))
        o_ref[0] = jnp.concatenate(outs, axis=1)


def _attn_decode(qn, k_new, v_new, lf_new, cache_k, cache_v, cache_logf, page_table, *, n_heads, pages_per_step):
    bsz, t, e_b = qn.shape
    n_pool, page, _, _ = cache_k.shape
    n_pages = page_table.shape[1]
    g = pages_per_step
    ht = n_heads * t
    gh = MXU_DEPTH // HEAD_DIM
    assert n_pages % g == 0 and g % 2 == 0 and ht == page == LANES and t % SUBLANES == 0 and n_heads % gh == 0
    ck = jnp.transpose(cache_k, (0, 2, 3, 1)).reshape(n_pool, e_b, page)
    cv = jnp.transpose(cache_v, (0, 2, 3, 1)).reshape(n_pool, e_b, page)
    clf = jnp.transpose(cache_logf, (0, 2, 1))
    expand = jnp.asarray((np.arange(n_heads)[:, None] == (np.arange(ht)[None, :] // t)).astype(np.float32), BF16)
    repeat = jnp.asarray((np.arange(ht)[:, None] % t == np.arange(t)[None, :]).astype(np.float32))
    tok = lambda width: pl.BlockSpec((1, t, width), lambda b, s, pt: (b, 0, 0))
    page_spec = lambda rows, i: pl.BlockSpec((None, rows, page), lambda b, s, pt, i=i: (pt[b, s * g + i], 0, 0))
    in_specs = ([tok(e_b), tok(e_b), tok(e_b), tok(n_heads),
                 pl.BlockSpec(expand.shape, lambda b, s, pt: (0, 0)),
                 pl.BlockSpec(repeat.shape, lambda b, s, pt: (0, 0))]
                + [page_spec(e_b, i) for i in range(g)] * 2
                + [page_spec(n_heads, i) for i in range(g)])
    return pl.pallas_call(
        functools.partial(_attn_decode_kernel, n_pages_step=g, n_heads=n_heads, t=t, group_heads=gh),
        grid_spec=pltpu.PrefetchScalarGridSpec(
            num_scalar_prefetch=1,
            grid=(bsz, n_pages // g),
            in_specs=in_specs,
            out_specs=tok(e_b),
            scratch_shapes=[pltpu.VMEM((ht, e_b), BF16), pltpu.VMEM((ht, 1), F32), pltpu.VMEM((ht, 1), F32),
                            pltpu.VMEM((n_heads // gh, gh * HEAD_DIM, LANES), F32),
                            pltpu.VMEM((n_heads, 1), F32)]),
        out_shape=jax.ShapeDtypeStruct((bsz, t, e_b), F32),
        compiler_params=_params(("arbitrary", "arbitrary")),
        name="attn_decode",
    )(page_table, qn, k_new, v_new, lf_new, expand, repeat, *([ck] * g), *([cv] * g), *([clf] * g))


def _fox_out_kernel(o_ref, zs_ref, x_ref, p_ref, wout_ref, gw_ref, pw_ref, y_ref):
    m = x_ref.shape[0] * x_ref.shape[1]
    two_d = lambda r: r[...].reshape(m, r.shape[-1])
    gated = (two_d(o_ref) * two_d(zs_ref)).astype(BF16)
    x1 = two_d(x_ref) + _wdot(gated, wout_ref)
    y_ref[...] = _ple_tail(x1, two_d(p_ref), gw_ref, pw_ref).reshape(y_ref.shape)


def _fox_out(o, zs, x, p_all, layer, w_out, gate_w, ple_w, *, tm):
    bsz, s, d = x.shape
    pd = p_all.shape[-1]
    if s % tm == 0:
        nb, ns = 1, tm
    else:
        nb, ns = bsz, s
    tok = lambda width: pl.BlockSpec((nb, ns, width), lambda b, i: (b, i, 0))
    weights = (w_out, gate_w, ple_w)
    return pl.pallas_call(
        _fox_out_kernel,
        grid=(bsz // nb, s // ns),
        in_specs=[tok(o.shape[-1]), tok(zs.shape[-1]), tok(d),
                  pl.BlockSpec((None, nb, ns, pd), lambda b, i: (layer, b, i, 0))]
                 + [_const_spec(w.shape) for w in weights],
        out_specs=tok(d),
        out_shape=jax.ShapeDtypeStruct((bsz, s, d), F32),
        compiler_params=_params(("arbitrary", "arbitrary")),
        name="fox_out",
    )(o, zs, x, p_all, *weights)


TM_PROMPT = 256
TQ_PROMPT = 512
ATTN_HEADS_PER_STEP = 2
PAGES_PER_STEP = 16


def kernel(x_prompt, x_sample, state_conv, cache_k, cache_v, cache_logf, page_table, p_prompt, p_sample,
           a_norm, a_w_in, a_conv_w, a_conv_b, a_ln_g, a_ln_b, a_w_out, kv_norm, kv_w, kv_k_norm, kv_f_bias,
           b_norm, b_w_q, b_q_norm, b_w_out, ple_w, ple_gate_w):
    n_a = a_norm.shape[0]
    n_b = b_norm.shape[0]
    n_heads = kv_f_bias.shape[0]
    e_b = n_heads * HEAD_DIM
    d = x_prompt.shape[-1]
    row = lambda v: v.reshape(1, -1)
    bf = _lane_tiles

    a_w_in16, a_w_out16, b_w_q16, b_w_out16 = bf(a_w_in), bf(a_w_out), bf(b_w_q), bf(b_w_out)
    ple_w16, gate_w16 = bf(ple_w), bf(ple_gate_w)
    w_kv16 = bf(kv_w[:, :2 * e_b])
    w_f16 = bf(jnp.pad(kv_w[:, 2 * e_b:], ((0, 0), (0, LANES - n_heads))))
    f_bias = jnp.pad(row(kv_f_bias), ((0, 0), (0, LANES - n_heads)))
    k_gain = jnp.tile(row(kv_k_norm), (1, n_heads))
    bd = _block_diag_ones(n_heads)
    tm = min(TM_PROMPT, x_prompt.shape[1])
    tq = min(TQ_PROMPT, x_prompt.shape[1])

    def trunk(x, p_all, prev, prompt):
        conv_states = []
        if prev is not None:
            prev = jnp.transpose(prev, (0, 2, 1, 3))
        for i in range(n_a):
            x, st = _conv_layer(x, p_all, i, prev, row(a_norm[i]), a_w_in16[i],
                                a_conv_w[i], row(a_conv_b[i]), row(a_ln_g[i]), row(a_ln_b[i]), a_w_out16[i],
                                gate_w16[i], ple_w16[i], tm=tm)
            conv_states.append(st)
        kv = _kv_proj(x, row(kv_norm), w_kv16, w_f16, f_bias, k_gain, bd, n_heads=n_heads, prompt=prompt, tm=tm)
        k_new, v_new, lf_new = kv[:3]
        for j in range(n_b):
            q_gain = jnp.tile(row(b_q_norm[j]), (1, n_heads)) * (LOG2E * HEAD_DIM ** -0.5)
            if prompt:
                qp, zs = _fox_q(x, kv[3], row(b_norm[j]), b_w_q16[j], q_gain, bd, n_heads=n_heads, prompt=True, tm=tm)
                o = _attn_prompt(qp, kv[4], kv[5], tq=tq, heads_per_step=ATTN_HEADS_PER_STEP)
            else:
                qn, zs = _fox_q(x, None, row(b_norm[j]), b_w_q16[j], q_gain, bd, n_heads=n_heads, prompt=False, tm=tm)
                o = _attn_decode(qn, k_new, v_new, lf_new, cache_k, cache_v, cache_logf, page_table,
                                 n_heads=n_heads, pages_per_step=min(PAGES_PER_STEP, page_table.shape[1]))
            x = _fox_out(o, zs, x, p_all, n_a + j, b_w_out16[j], gate_w16[n_a + j], ple_w16[n_a + j], tm=tm)
        shape4 = k_new.shape[:2] + (n_heads, HEAD_DIM)
        conv_state = jnp.transpose(jnp.stack(conv_states, axis=0), (0, 2, 1, 3))
        return x, conv_state, k_new.reshape(shape4), v_new.reshape(shape4), lf_new

    y_p, conv_p, k_p, v_p, lf_p = trunk(x_prompt, p_prompt, None, True)
    y_s, conv_s, k_s, v_s, lf_s = trunk(x_sample, p_sample, state_conv, False)
    return (y_p, y_s, conv_p, k_p, v_p, lf_p, conv_s, k_s, v_s, lf_s)
```

```python
import functools
import math

import numpy as np
import jax
import jax.numpy as jnp
from jax import lax
from jax.experimental import pallas as pl
from jax.experimental.pallas import tpu as pltpu

F32 = jnp.float32
BF16 = jnp.bfloat16

EPS = 1e-6
CONV_W = 31
HALO = CONV_W - 1
HEAD_DIM = 64
LANES = 128
SUBLANES = 8
MXU_DEPTH = 256
HALO_PAD = 32
CONV_TIME_CHUNK = 8
CONV_LANE_CHUNK = 2 * LANES
LOG2E = math.log2(math.e)
NEG_BIG = -1e30
VMEM_LIMIT = 56 * 1024 * 1024


def _dot(a, b):
    return jnp.dot(a, b, preferred_element_type=F32)


def _dot_nt(a, b):
    return lax.dot_general(a, b, (((1,), (1,)), ((), ())), preferred_element_type=F32)


def _lane_tiles(w):
    *lead, k, n = w.shape
    return jnp.moveaxis(w.astype(BF16).reshape(*lead, k, n // LANES, LANES), -2, -3)


def _wdot(a, w_ref, col0=0, ncols=None, lead=()):
    first = col0 // LANES
    count = w_ref.shape[-3] - first if ncols is None else ncols // LANES
    tiles = [w_ref[lead + (j,)] for j in range(first, first + count)]
    return _dot(a, tiles[0] if count == 1 else jnp.concatenate(tiles, axis=1))


def _split3(x):
    p0 = x.astype(BF16)
    r = x - p0.astype(F32)
    p1 = r.astype(BF16)
    r = r - p1.astype(F32)
    return p0, p1, r.astype(BF16)


def _split2(x):
    p0 = x.astype(BF16)
    return p0, (x - p0.astype(F32)).astype(BF16)


def _rms(x, g):
    return x * lax.rsqrt(jnp.mean(x * x, axis=-1, keepdims=True) + EPS) * g


def _sigmoid(x):
    return 1.0 / (1.0 + jnp.exp(-x))


def _silu(x):
    return x * _sigmoid(x)


def _log_sigmoid(x):
    return jnp.minimum(x, 0.0) - jnp.log1p(jnp.exp(-jnp.abs(x)))


def _const_spec(shape):
    nd = len(shape)
    return pl.BlockSpec(shape, lambda *_: (0,) * nd, pipeline_mode=pl.Buffered(1))


def _params(sem):
    return pltpu.CompilerParams(dimension_semantics=sem, vmem_limit_bytes=VMEM_LIMIT)


def _ple_tail(x1, p, gw_ref, pw_ref):
    gate = _sigmoid(_wdot(x1.astype(BF16), gw_ref))
    return x1 + gate * _wdot(p.astype(BF16), pw_ref)


def _conv_layer_kernel(*refs, nb, t, time_chunk, lane_chunk, has_prev):
    if has_prev:
        (x_ref, p_ref, prev_ref, g_ref, win_ref, cw_ref, cb_ref, lng_ref, lnb_ref, wout_ref,
         gw_ref, pw_ref, y_ref, st_ref, xs_ref, ps_ref, ys_ref, uext_ref, c_ref, zs_ref) = refs
    else:
        (x_ref, p_ref, g_ref, win_ref, cw_ref, cb_ref, lng_ref, lnb_ref, wout_ref,
         gw_ref, pw_ref, y_ref, st_ref, xs_ref, ps_ref, ys_ref, uext_ref, c_ref, zs_ref) = refs
    e = cw_ref.shape[-1]

    for tt in range(t):
        xs_ref[pl.ds(tt * nb, nb), :] = x_ref[:, tt, :]
        ps_ref[pl.ds(tt * nb, nb), :] = p_ref[:, tt, :]

    if has_prev:
        uext_ref[pl.ds(HALO_PAD - HALO, HALO)] = prev_ref[...]
    else:
        s = pl.program_id(1)

        @pl.when(s == 0)
        def _():
            uext_ref[pl.ds(0, HALO_PAD)] = jnp.zeros((HALO_PAD, nb, e), F32)

        @pl.when(s > 0)
        def _():
            uext_ref[pl.ds(0, HALO_PAD)] = uext_ref[pl.ds(t, HALO_PAD)]

    h = _rms(xs_ref[...], g_ref[...]).astype(BF16)
    for l0 in range(0, e, lane_chunk):
        lanes = pl.ds(l0, lane_chunk)
        a = _wdot(h, win_ref, l0, lane_chunk)
        b = _wdot(h, win_ref, e + l0, lane_chunk)
        uext_ref[pl.ds(HALO_PAD, t), :, lanes] = (a * _sigmoid(b)).reshape(t, nb, lane_chunk)
        for t0 in range(0, t, time_chunk):
            for b0 in range(0, nb, SUBLANES):
                acc = jnp.broadcast_to(cb_ref[:, lanes], (time_chunk, SUBLANES, lane_chunk))
                for k in range(CONV_W):
                    win = uext_ref[pl.ds(HALO_PAD - HALO + t0 + k, time_chunk), pl.ds(b0, SUBLANES), lanes]
                    acc = acc + win * cw_ref[k, :, lanes]
                c_ref[pl.ds(t0, time_chunk), pl.ds(b0, SUBLANES), lanes] = acc
        zs_ref[:, lanes] = _silu(_wdot(h, win_ref, 2 * e + l0, lane_chunk))

    if has_prev:
        st_ref[...] = uext_ref[pl.ds(HALO_PAD + t - HALO, HALO)]
    else:
        @pl.when(pl.program_id(1) == pl.num_programs(1) - 1)
        def _():
            st_ref[...] = uext_ref[pl.ds(HALO_PAD + t - HALO, HALO)]

    th = t // 2
    for half in range(2):
        rows = pl.ds(half * th * nb, th * nb)
        c = c_ref[pl.ds(half * th, th)].reshape(th * nb, e)
        mu = jnp.mean(c, axis=-1, keepdims=True)
        cc = c - mu
        ln = cc * lax.rsqrt(jnp.mean(cc * cc, axis=-1, keepdims=True) + EPS) * lng_ref[...] + lnb_ref[...]
        gated = (_silu(ln) * zs_ref[rows, :]).astype(BF16)
        x1 = xs_ref[rows, :] + _wdot(gated, wout_ref)
        ys_ref[rows, :] = _ple_tail(x1, ps_ref[rows, :], gw_ref, pw_ref)
    for tt in range(t):
        y_ref[:, tt, :] = ys_ref[pl.ds(tt * nb, nb), :]


def _conv_layer(x, p_all, layer, prev, g, w_in, conv_w, conv_b, ln_g, ln_b, w_out, gate_w, ple_w, *, tm):
    bsz, s, d = x.shape
    e = conv_w.shape[1]
    pd = p_all.shape[-1]
    has_prev = prev is not None
    conv_w8 = jnp.broadcast_to(conv_w[:, None, :], (CONV_W, SUBLANES, e))
    conv_b8 = jnp.broadcast_to(conv_b, (SUBLANES, e))
    weights = (g, w_in, conv_w8, conv_b8, ln_g, ln_b, w_out, gate_w, ple_w)
    if has_prev:
        nb, t = bsz, s
    else:
        nb, t = SUBLANES, tm // SUBLANES
        assert s % t == 0
    time_chunk = min(t, CONV_TIME_CHUNK)
    assert bsz % nb == 0 and nb % SUBLANES == 0 and t % time_chunk == 0 and t % 2 == 0
    x_spec = pl.BlockSpec((nb, t, d), lambda gb, i: (gb, i, 0))
    st_spec = pl.BlockSpec((HALO, nb, e), lambda gb, i: (0, gb, 0))
    in_specs = [x_spec, pl.BlockSpec((None, nb, t, pd), lambda gb, i: (layer, gb, i, 0))]
    args = (x, p_all)
    if has_prev:
        in_specs.append(pl.BlockSpec((None, HALO, nb, e), lambda gb, i: (layer, 0, gb, 0)))
        args += (prev,)
    kern = functools.partial(_conv_layer_kernel, nb=nb, t=t, time_chunk=time_chunk,
                             lane_chunk=CONV_LANE_CHUNK, has_prev=has_prev)
    m = nb * t
    return pl.pallas_call(
        kern,
        grid=(bsz // nb, s // t),
        in_specs=in_specs + [_const_spec(w.shape) for w in weights],
        out_specs=[x_spec, st_spec],
        out_shape=[jax.ShapeDtypeStruct((bsz, s, d), F32), jax.ShapeDtypeStruct((HALO, bsz, e), F32)],
        scratch_shapes=[pltpu.VMEM((m, d), F32), pltpu.VMEM((m, pd), F32), pltpu.VMEM((m, d), F32),
                        pltpu.VMEM((HALO_PAD + t, nb, e), F32), pltpu.VMEM((t, nb, e), F32),
                        pltpu.VMEM((m, e), F32)],
        compiler_params=_params(("arbitrary", "arbitrary")),
        name="conv_layer_decode" if has_prev else "conv_layer_prompt",
    )(*args, *weights)


def _head_norm(raw, bd_ref, gain):
    ms = _wdot((raw * raw).astype(BF16), bd_ref) * (1.0 / HEAD_DIM)
    return raw * lax.rsqrt(ms + EPS) * gain


def _head_tiles(arr, n_heads, lane, extra):
    for hd in range(n_heads):
        pair = arr[:, (hd // 2) * LANES:(hd // 2 + 1) * LANES]
        if hd % 2 == 1:
            pair = pltpu.roll(pair, HEAD_DIM, axis=1)
        yield hd, jnp.where(lane < HEAD_DIM, pair, extra(hd))


def _bias_lanes(c, lane, piece_lane, ones_lane):
    pieces = [p.astype(F32) for p in _split3(c)]
    base = jnp.where((lane >= ones_lane) & (lane < ones_lane + 3), 1.0, 0.0)

    def extra(hd):
        tile = base
        for i, piece in enumerate(pieces):
            tile = jnp.where(lane == piece_lane + i, pltpu.roll(piece, (piece_lane + i - hd) % LANES, axis=1), tile)
        return tile

    return extra


def _kv_kernel(*refs, n_heads, prompt, n_parts):
    if prompt:
        (x_ref, g_ref, wkv_ref, wf_ref, fb_ref, kg_ref, bd_ref,
         k_ref, v_ref, lf_ref, c_ref, kp_ref, vp_ref, carry_ref) = refs

        @pl.when(pl.program_id(1) == 0)
        def _():
            carry_ref[...] = jnp.zeros(carry_ref.shape, F32)
    else:
        (x_ref, g_ref, wkv_ref, wf_ref, fb_ref, kg_ref, bd_ref, k_ref, v_ref, lf_ref) = refs
    e_b = n_heads * HEAD_DIM
    m = x_ref.shape[-2] // n_parts
    row = lax.broadcasted_iota(jnp.int32, (m, m), 0)
    col = lax.broadcasted_iota(jnp.int32, (m, m), 1)
    tri = jnp.where(col <= row, 1.0, 0.0).astype(BF16)
    lane = lax.broadcasted_iota(jnp.int32, (m, LANES), 1)
    ones_col = jnp.where(lane == HEAD_DIM, 1.0, 0.0)
    for part in range(n_parts):
        rows = pl.ds(part * m, m)
        h = _rms(x_ref[0, rows, :], g_ref[...]).astype(BF16)
        kn = _head_norm(_wdot(h, wkv_ref, 0, e_b), bd_ref, kg_ref[...])
        v = _wdot(h, wkv_ref, e_b, e_b)
        logf = _log_sigmoid(_wdot(h, wf_ref) + fb_ref[...])
        k_ref[0, rows, :] = kn
        v_ref[0, rows, :] = v
        lf_ref[0, rows, :] = logf[:, :n_heads]
        if not prompt:
            continue
        c = carry_ref[...]
        for piece in _split3(logf * LOG2E):
            c = c + _dot(tri, piece)
        carry_ref[...] = c[m - 1:m, :]
        c_ref[0, rows, :] = c
        for hd, tile in _head_tiles(kn, n_heads, lane, _bias_lanes(-c, lane, HEAD_DIM, HEAD_DIM + 3)):
            kp_ref[0, hd, rows, :] = tile.astype(BF16)
        for hd, tile in _head_tiles(v, n_heads, lane, lambda hd: ones_col):
            vp_ref[0, hd, rows, :] = tile.astype(BF16)


def _block_diag_ones(n_heads):
    idx = np.arange(n_heads * HEAD_DIM) // HEAD_DIM
    return _lane_tiles(jnp.asarray((idx[:, None] == idx[None, :]).astype(np.float32)))


def _kv_proj(x, g, w_kv, w_f, f_bias, k_gain, bd, *, n_heads, prompt, tm):
    bsz, s, d = x.shape
    e_b = n_heads * HEAD_DIM
    outs = [jax.ShapeDtypeStruct((bsz, s, e_b), F32), jax.ShapeDtypeStruct((bsz, s, e_b), F32),
            jax.ShapeDtypeStruct((bsz, s, n_heads), F32)]
    weights = [g, w_kv, w_f, f_bias, k_gain, bd]
    if prompt:
        assert s % tm == 0
        grid = (bsz, s // tm)
        tok = lambda width: pl.BlockSpec((1, tm, width), lambda b, i: (b, i, 0))
        head_spec = pl.BlockSpec((1, n_heads, tm, LANES), lambda b, i: (b, 0, i, 0))
        out_specs = [tok(e_b), tok(e_b), tok(n_heads), tok(LANES), head_spec, head_spec]
        outs += [jax.ShapeDtypeStruct((bsz, s, LANES), F32),
                 jax.ShapeDtypeStruct((bsz, n_heads, s, LANES), BF16),
                 jax.ShapeDtypeStruct((bsz, n_heads, s, LANES), BF16)]
        x_spec = tok(d)
        scratch = [pltpu.VMEM((1, LANES), F32)]
        sem = ("arbitrary", "arbitrary")
    else:
        grid = (1,)
        x2 = x.reshape(1, bsz * s, d)
        full = lambda width: pl.BlockSpec((1, bsz * s, width), lambda i: (0, 0, 0))
        outs = [jax.ShapeDtypeStruct((1, bsz * s, o.shape[-1]), F32) for o in outs]
        out_specs = [full(e_b), full(e_b), full(n_heads)]
        x_spec = full(d)
        x = x2
        scratch = []
        sem = ("arbitrary",)
    res = pl.pallas_call(
        functools.partial(_kv_kernel, n_heads=n_heads, prompt=prompt, n_parts=KV_PARTS if prompt else 1),
        grid=grid,
        in_specs=[x_spec] + [_const_spec(w.shape) for w in weights],
        out_specs=out_specs,
        out_shape=outs,
        scratch_shapes=scratch,
        compiler_params=_params(sem),
        name="kv_proj_prompt" if prompt else "kv_proj_decode",
    )(x, *weights)
    if not prompt:
        res = [r.reshape(bsz, s, r.shape[-1]) for r in res]
    return res


def _fox_q_kernel(*refs, n_heads, prompt):
    if prompt:
        x_ref, c_ref, g_ref, wq_ref, qg_ref, bd_ref, qp_ref, zs_ref = refs
    else:
        x_ref, g_ref, wq_ref, qg_ref, bd_ref, qn_ref, zs_ref = refs
    e_b = n_heads * HEAD_DIM
    m = x_ref.shape[-2]
    x = x_ref[...].reshape(m, x_ref.shape[-1])
    h = _rms(x, g_ref[...]).astype(BF16)
    qn = _head_norm(_wdot(h, wq_ref, 0, e_b), bd_ref, qg_ref[...])
    zs_ref[...] = _silu(_wdot(h, wq_ref, e_b, e_b)).reshape(zs_ref.shape)
    if not prompt:
        qn_ref[...] = qn.reshape(qn_ref.shape)
        return
    lane = lax.broadcasted_iota(jnp.int32, (m, LANES), 1)
    extra = _bias_lanes(c_ref[...].reshape(m, LANES), lane, HEAD_DIM + 3, HEAD_DIM)
    for hd, tile in _head_tiles(qn, n_heads, lane, extra):
        qp_ref[0, hd] = tile.astype(BF16)


def _fox_q(x, c, g, w_q, q_gain, bd, *, n_heads, prompt, tm):
    bsz, s, d = x.shape
    e_b = n_heads * HEAD_DIM
    if prompt:
        grid = (bsz, s // tm)
        tok = lambda width: pl.BlockSpec((1, tm, width), lambda b, i: (b, i, 0))
        weights = [g, w_q, q_gain, bd]
        args = (x, c)
        in_specs = [tok(d), tok(LANES)]
        out_specs = [pl.BlockSpec((1, n_heads, tm, LANES), lambda b, i: (b, 0, i, 0)), tok(e_b)]
        outs = [jax.ShapeDtypeStruct((bsz, n_heads, s, LANES), BF16), jax.ShapeDtypeStruct((bsz, s, e_b), F32)]
        sem = ("arbitrary", "arbitrary")
    else:
        grid = (1,)
        full = lambda width: pl.BlockSpec((1, bsz * s, width), lambda i: (0, 0, 0))
        weights = [g, w_q, q_gain, bd]
        args = (x.reshape(1, bsz * s, d),)
        in_specs = [full(d)]
        out_specs = [full(e_b), full(e_b)]
        outs = [jax.ShapeDtypeStruct((1, bsz * s, e_b), F32)] * 2
        sem = ("arbitrary",)
    res = pl.pallas_call(
        functools.partial(_fox_q_kernel, n_heads=n_heads, prompt=prompt),
        grid=grid,
        in_specs=in_specs + [_const_spec(w.shape) for w in weights],
        out_specs=out_specs,
        out_shape=outs,
        compiler_params=_params(sem),
        name="fox_q_prompt" if prompt else "fox_q_decode",
    )(*args, *weights)
    if not prompt:
        res = [r.reshape(bsz, s, e_b) for r in res]
    return res


def _attn_prompt_kernel(q_ref, k_ref, v_ref, o_ref, *, tq):
    n_h, s_len = q_ref.shape[1], q_ref.shape[2]
    th = tq // 2
    lane = lax.broadcasted_iota(jnp.int32, (tq, LANES), 1)
    tri_full = (lax.broadcasted_iota(jnp.int32, (tq, th), 1) <= lax.broadcasted_iota(jnp.int32, (tq, th), 0))
    tri_half = tri_full[:th]

    def fold(q, hh, k0, nk, m_old, acc, mask):
        s = _dot_nt(q, k_ref[0, hh, pl.ds(k0, nk), :])
        if mask is not None:
            s = jnp.where(mask, s, NEG_BIG)
        m_new = jnp.maximum(m_old, jnp.max(s, axis=-1, keepdims=True))
        p = jnp.exp2(s - m_new).astype(BF16)
        pv = _dot(p, v_ref[0, hh, pl.ds(k0, nk), :])
        return m_new, jnp.exp2(m_old - m_new) * acc + pv

    for qi in range(s_len // tq):
        q0 = qi * tq
        outs = []
        state = [(jnp.full((tq, 1), NEG_BIG, F32), jnp.zeros((tq, LANES), F32)) for _ in range(n_h)]
        for j in range(qi):
            for hh in range(n_h):
                state[hh] = fold(q_ref[0, hh, pl.ds(q0, tq), :], hh, j * tq, tq, *state[hh], None)
        for hh in range(n_h):
            m, acc = fold(q_ref[0, hh, pl.ds(q0, tq), :], hh, q0, th, *state[hh], tri_full)
            m_lo, acc_lo = fold(q_ref[0, hh, pl.ds(q0 + th, th), :], hh, q0 + th, th, m[th:], acc[th:], tri_half)
            acc = jnp.concatenate([acc[:th], acc_lo], axis=0)
            denom = jnp.sum(jnp.where(lane == HEAD_DIM, acc, 0.0), axis=-1, keepdims=True)
            outs.append(acc / denom)
        for pair in range(n_h // 2):
            o_ref[0, pl.ds(q0, tq), pl.ds(pair * LANES, LANES)] = jnp.where(
                lane < HEAD_DIM, outs[2 * pair], pltpu.roll(outs[2 * pair + 1], HEAD_DIM, axis=1))


def _attn_prompt(qp, kp, vp, *, tq, heads_per_step):
    bsz, n_heads, s, _ = qp.shape
    hps = heads_per_step
    assert s % tq == 0 and n_heads % hps == 0 and hps % 2 == 0 and tq % (2 * SUBLANES) == 0
    head_spec = pl.BlockSpec((1, hps, s, LANES), lambda b, hp: (b, hp, 0, 0))
    return pl.pallas_call(
        functools.partial(_attn_prompt_kernel, tq=tq),
        grid=(bsz, n_heads // hps),
        in_specs=[head_spec, head_spec, head_spec],
        out_specs=pl.BlockSpec((1, s, hps * HEAD_DIM), lambda b, hp: (b, 0, hp)),
        out_shape=jax.ShapeDtypeStruct((bsz, s, n_heads * HEAD_DIM), F32),
        compiler_params=_params(("arbitrary", "arbitrary")),
        name="attn_prompt",
    )(qp, kp, vp)


def _attn_decode_kernel(*refs, n_pages_step, n_heads, t, group_heads):
    g = n_pages_step
    qn_ref, kn_ref, vn_ref, lfn_ref, exp_ref, rep_ref = refs[1:7]
    k_refs = refs[7:7 + g]
    v_refs = refs[7 + g:7 + 2 * g]
    lf_refs = refs[7 + 2 * g:7 + 3 * g]
    o_ref = refs[7 + 3 * g]
    qexp_ref, m_ref, l_ref, acc_ref, carry_ref = refs[8 + 3 * g:]
    ht = n_heads * t
    e_b = n_heads * HEAD_DIM
    page = k_refs[0].shape[1]
    s_idx = pl.program_id(1)

    @pl.when(s_idx == 0)
    def _():
        spread = _dot(rep_ref[...], qn_ref[0].astype(BF16).astype(F32))
        keep = (lax.broadcasted_iota(jnp.int32, (ht, e_b), 0) // t
                == lax.broadcasted_iota(jnp.int32, (ht, e_b), 1) // HEAD_DIM)
        qexp_ref[...] = jnp.where(keep, spread, 0.0).astype(BF16)
        m_ref[...] = jnp.full(m_ref.shape, NEG_BIG, F32)
        l_ref[...] = jnp.zeros(l_ref.shape, F32)
        acc_ref[...] = jnp.zeros(acc_ref.shape, F32)
        carry_ref[...] = jnp.zeros(carry_ref.shape, F32)

    row = lax.broadcasted_iota(jnp.int32, (page, page), 0)
    col = lax.broadcasted_iota(jnp.int32, (page, page), 1)
    upper = jnp.where(row <= col, 1.0, 0.0).astype(BF16)

    def rows_per_head(a):
        return jnp.broadcast_to(a[:, None, :], (n_heads, t, a.shape[-1])).reshape(ht, a.shape[-1])

    def cumsum_keys(c, lf):
        for piece in _split3(lf):
            c = c + _dot(piece, upper)
        return c

    def page_bias(lf_t):
        lf2 = lf_t * LOG2E
        before = carry_ref[...]
        carry_ref[...] = before + jnp.sum(lf2, axis=1, keepdims=True)
        return cumsum_keys(jnp.broadcast_to(before, (n_heads, page)), lf2)

    gr, gc = group_heads * t, group_heads * HEAD_DIM
    n_groups = n_heads // group_heads
    rows = lambda a, c: a[c * gr:(c + 1) * gr]
    q_group = lambda c: qexp_ref[pl.ds(c * gr, gr), pl.ds(c * gc, gc)]
    slab = lambda ref, c: ref[pl.ds(c * gc, gc), :].astype(BF16)
    by_group = lambda f: jnp.concatenate([f(c) for c in range(n_groups)], axis=0)

    def update(s, pv_t):
        m_old = m_ref[...]
        m_new = jnp.maximum(m_old, jnp.max(s, axis=1, keepdims=True))
        p = jnp.exp2(s - m_new)
        alpha = jnp.exp2(m_old - m_new)
        m_ref[...] = m_new
        l_ref[...] = alpha * l_ref[...] + jnp.sum(p, axis=1, keepdims=True)
        alpha_row = jnp.broadcast_to(alpha, (ht, ht)).T[0:1, :]
        p = p.astype(BF16)
        for c in range(n_groups):
            acc_ref[c, :, pl.ds(0, gr)] = (alpha_row[:, c * gr:(c + 1) * gr] * acc_ref[c, :, pl.ds(0, gr)]
                                           + pv_t(p, c))

    s = jnp.concatenate(
        [by_group(lambda c: _dot(q_group(c), slab(k_refs[i], c))) - rows_per_head(page_bias(lf_refs[i][...]))
         for i in range(g)], axis=1)
    update(s, lambda p, c: sum(
        _dot_nt(jnp.concatenate([slab(v_refs[i], c), slab(v_refs[i + 1], c)], axis=1),
                rows(p, c)[:, i * page:(i + 2) * page]) for i in range(0, g, 2)))

    @pl.when(s_idx == pl.num_programs(1) - 1)
    def _():
        pad = lambda a: jnp.concatenate([a, jnp.zeros((page - t, a.shape[1]), F32)], axis=0)
        lexp = jnp.zeros((page, ht), F32)
        for piece in _split3(pad(lfn_ref[0]) * LOG2E):
            lexp = lexp + _dot(piece, exp_ref[...])
        c_new = cumsum_keys(rows_per_head(jnp.broadcast_to(carry_ref[...], (n_heads, page))), lexp.T)
        k_pad = pad(kn_ref[0]).astype(BF16)
        v_pad = pad(vn_ref[0]).astype(BF16)
        s_new = by_group(lambda c: _dot_nt(q_group(c), k_pad[:, c * gc:(c + 1) * gc])) - c_new
        key = lax.broadcasted_iota(jnp.int32, (ht, page), 1)
        tok = lax.broadcasted_iota(jnp.int32, (ht, page), 0) % t
        tall = lambda a: jnp.concatenate([a, jnp.zeros((ht - a.shape[0], a.shape[1]), F32)], axis=0)
        update(jnp.where(key <= tok, s_new, NEG_BIG),
               lambda p, c: tall(_dot(rows(p, c), v_pad[:, c * gc:(c + 1) * gc])).T[:, :gr])
        keep = (lax.broadcasted_iota(jnp.int32, (gr, gc), 0) // t
                == lax.broadcasted_iota(jnp.int32, (gr, gc), 1) // HEAD_DIM)
        outs = []
        for c in range(n_groups):
            o = acc_ref[c].T[:gr, :] / rows(l_ref[...], c)
            outs.append(jnp.sum(jnp.where(keep, o, 0.0).reshape(group_heads, t, gc), axis=0))
        o_ref[0] = jnp.concatenate(outs, axis=1)


def _attn_decode(qn, k_new, v_new, lf_new, cache_k, cache_v, cache_logf, page_table, *, n_heads, pages_per_step):
    bsz, t, e_b = qn.shape
    n_pool, page, _, _ = cache_k.shape
    n_pages = page_table.shape[1]
    g = pages_per_step
    ht = n_heads * t
    gh = MXU_DEPTH // HEAD_DIM
    assert n_pages % g == 0 and g % 2 == 0 and ht == page == LANES and t % SUBLANES == 0 and n_heads % gh == 0
    ck = jnp.transpose(cache_k, (0, 2, 3, 1)).reshape(n_pool, e_b, page)
    cv = jnp.transpose(cache_v, (0, 2, 3, 1)).reshape(n_pool, e_b, page)
    clf = jnp.transpose(cache_logf, (0, 2, 1))
    expand = jnp.asarray((np.arange(n_heads)[:, None] == (np.arange(ht)[None, :] // t)).astype(np.float32), BF16)
    repeat = jnp.asarray((np.arange(ht)[:, None] % t == np.arange(t)[None, :]).astype(np.float32))
    tok = lambda width: pl.BlockSpec((1, t, width), lambda b, s, pt: (b, 0, 0))
    page_spec = lambda rows, i: pl.BlockSpec((None, rows, page), lambda b, s, pt, i=i: (pt[b, s * g + i], 0, 0))
    in_specs = ([tok(e_b), tok(e_b), tok(e_b), tok(n_heads),
                 pl.BlockSpec(expand.shape, lambda b, s, pt: (0, 0)),
                 pl.BlockSpec(repeat.shape, lambda b, s, pt: (0, 0))]
                + [page_spec(e_b, i) for i in range(g)] * 2
                + [page_spec(n_heads, i) for i in range(g)])
    return pl.pallas_call(
        functools.partial(_attn_decode_kernel, n_pages_step=g, n_heads=n_heads, t=t, group_heads=gh),
        grid_spec=pltpu.PrefetchScalarGridSpec(
            num_scalar_prefetch=1,
            grid=(bsz, n_pages // g),
            in_specs=in_specs,
            out_specs=tok(e_b),
            scratch_shapes=[pltpu.VMEM((ht, e_b), BF16), pltpu.VMEM((ht, 1), F32), pltpu.VMEM((ht, 1), F32),
                            pltpu.VMEM((n_heads // gh, gh * HEAD_DIM, LANES), F32),
                            pltpu.VMEM((n_heads, 1), F32)]),
        out_shape=jax.ShapeDtypeStruct((bsz, t, e_b), F32),
        compiler_params=_params(("arbitrary", "arbitrary")),
        name="attn_decode",
    )(page_table, qn, k_new, v_new, lf_new, expand, repeat, *([ck] * g), *([cv] * g), *([clf] * g))


def _fox_out_kernel(o_ref, zs_ref, x_ref, p_ref, wout_ref, gw_ref, pw_ref, y_ref):
    m = x_ref.shape[0] * x_ref.shape[1]
    two_d = lambda r: r[...].reshape(m, r.shape[-1])
    gated = (two_d(o_ref) * two_d(zs_ref)).astype(BF16)
    x1 = two_d(x_ref) + _wdot(gated, wout_ref)
    y_ref[...] = _ple_tail(x1, two_d(p_ref), gw_ref, pw_ref).reshape(y_ref.shape)


def _fox_out(o, zs, x, p_all, layer, w_out, gate_w, ple_w, *, tm):
    bsz, s, d = x.shape
    pd = p_all.shape[-1]
    if s % tm == 0:
        nb, ns = 1, tm
    else:
        nb, ns = bsz, s
    tok = lambda width: pl.BlockSpec((nb, ns, width), lambda b, i: (b, i, 0))
    weights = (w_out, gate_w, ple_w)
    return pl.pallas_call(
        _fox_out_kernel,
        grid=(bsz // nb, s // ns),
        in_specs=[tok(o.shape[-1]), tok(zs.shape[-1]), tok(d),
                  pl.BlockSpec((None, nb, ns, pd), lambda b, i: (layer, b, i, 0))]
                 + [_const_spec(w.shape) for w in weights],
        out_specs=tok(d),
        out_shape=jax.ShapeDtypeStruct((bsz, s, d), F32),
        compiler_params=_params(("arbitrary", "arbitrary")),
        name="fox_out",
    )(o, zs, x, p_all, *weights)


TM_PROMPT = 256
TM_KV = 512
KV_PARTS = 4
TQ_PROMPT = 512
ATTN_HEADS_PER_STEP = 2
PAGES_PER_STEP = 16


def kernel(x_prompt, x_sample, state_conv, cache_k, cache_v, cache_logf, page_table, p_prompt, p_sample,
           a_norm, a_w_in, a_conv_w, a_conv_b, a_ln_g, a_ln_b, a_w_out, kv_norm, kv_w, kv_k_norm, kv_f_bias,
           b_norm, b_w_q, b_q_norm, b_w_out, ple_w, ple_gate_w):
    n_a = a_norm.shape[0]
    n_b = b_norm.shape[0]
    n_heads = kv_f_bias.shape[0]
    e_b = n_heads * HEAD_DIM
    d = x_prompt.shape[-1]
    row = lambda v: v.reshape(1, -1)
    bf = _lane_tiles

    a_w_in16, a_w_out16, b_w_q16, b_w_out16 = bf(a_w_in), bf(a_w_out), bf(b_w_q), bf(b_w_out)
    ple_w16, gate_w16 = bf(ple_w), bf(ple_gate_w)
    w_kv16 = bf(kv_w[:, :2 * e_b])
    w_f16 = bf(jnp.pad(kv_w[:, 2 * e_b:], ((0, 0), (0, LANES - n_heads))))
    f_bias = jnp.pad(row(kv_f_bias), ((0, 0), (0, LANES - n_heads)))
    k_gain = jnp.tile(row(kv_k_norm), (1, n_heads))
    bd = _block_diag_ones(n_heads)
    tm = min(TM_PROMPT, x_prompt.shape[1])
    tq = min(TQ_PROMPT, x_prompt.shape[1])

    def trunk(x, p_all, prev, prompt):
        conv_states = []
        if prev is not None:
            prev = jnp.transpose(prev, (0, 2, 1, 3))
        for i in range(n_a):
            x, st = _conv_layer(x, p_all, i, prev, row(a_norm[i]), a_w_in16[i],
                                a_conv_w[i], row(a_conv_b[i]), row(a_ln_g[i]), row(a_ln_b[i]), a_w_out16[i],
                                gate_w16[i], ple_w16[i], tm=tm)
            conv_states.append(st)
        kv = _kv_proj(x, row(kv_norm), w_kv16, w_f16, f_bias, k_gain, bd, n_heads=n_heads, prompt=prompt,
                      tm=min(TM_KV, x.shape[1]))
        k_new, v_new, lf_new = kv[:3]
        for j in range(n_b):
            q_gain = jnp.tile(row(b_q_norm[j]), (1, n_heads)) * (LOG2E * HEAD_DIM ** -0.5)
            if prompt:
                qp, zs = _fox_q(x, kv[3], row(b_norm[j]), b_w_q16[j], q_gain, bd, n_heads=n_heads, prompt=True, tm=tm)
                o = _attn_prompt(qp, kv[4], kv[5], tq=tq, heads_per_step=ATTN_HEADS_PER_STEP)
            else:
                qn, zs = _fox_q(x, None, row(b_norm[j]), b_w_q16[j], q_gain, bd, n_heads=n_heads, prompt=False, tm=tm)
                o = _attn_decode(qn, k_new, v_new, lf_new, cache_k, cache_v, cache_logf, page_table,
                                 n_heads=n_heads, pages_per_step=min(PAGES_PER_STEP, page_table.shape[1]))
            x = _fox_out(o, zs, x, p_all, n_a + j, b_w_out16[j], gate_w16[n_a + j], ple_w16[n_a + j], tm=tm)
        shape4 = k_new.shape[:2] + (n_heads, HEAD_DIM)
        conv_state = jnp.transpose(jnp.stack(conv_states, axis=0), (0, 2, 1, 3))
        return x, conv_state, k_new.reshape(shape4), v_new.reshape(shape4), lf_new

    y_p, conv_p, k_p, v_p, lf_p = trunk(x_prompt, p_prompt, None, True)
    y_s, conv_s, k_s, v_s, lf_s = trunk(x_sample, p_sample, state_conv, False)
    return (y_p, y_s, conv_p, k_p, v_p, lf_p, conv_s, k_s, v_s, lf_s)
```

```python
import functools
import math

import numpy as np
import jax
import jax.numpy as jnp
from jax import lax
from jax.experimental import pallas as pl
from jax.experimental.pallas import tpu as pltpu

F32 = jnp.float32
BF16 = jnp.bfloat16

EPS = 1e-6
CONV_W = 31
HALO = CONV_W - 1
HEAD_DIM = 64
LANES = 128
SUBLANES = 8
MXU_DEPTH = 256
HALO_PAD = 32
CONV_TIME_CHUNK = 8
CONV_LANE_CHUNK = 2 * LANES
LOG2E = math.log2(math.e)
NEG_BIG = -1e30
VMEM_LIMIT = 56 * 1024 * 1024


def _dot(a, b):
    return jnp.dot(a, b, preferred_element_type=F32)


def _dot_nt(a, b):
    return lax.dot_general(a, b, (((1,), (1,)), ((), ())), preferred_element_type=F32)


def _lane_tiles(w):
    *lead, k, n = w.shape
    return jnp.moveaxis(w.astype(BF16).reshape(*lead, k, n // LANES, LANES), -2, -3)


def _wdot(a, w_ref, col0=0, ncols=None, lead=()):
    first = col0 // LANES
    count = w_ref.shape[-3] - first if ncols is None else ncols // LANES
    tiles = [w_ref[lead + (j,)] for j in range(first, first + count)]
    return _dot(a, tiles[0] if count == 1 else jnp.concatenate(tiles, axis=1))


def _split3(x):
    p0 = x.astype(BF16)
    r = x - p0.astype(F32)
    p1 = r.astype(BF16)
    r = r - p1.astype(F32)
    return p0, p1, r.astype(BF16)


def _split2(x):
    p0 = x.astype(BF16)
    return p0, (x - p0.astype(F32)).astype(BF16)


def _rms(x, g):
    return x * lax.rsqrt(jnp.mean(x * x, axis=-1, keepdims=True) + EPS) * g


def _sigmoid(x):
    return 1.0 / (1.0 + jnp.exp(-x))


def _silu(x):
    return x * _sigmoid(x)


def _log_sigmoid(x):
    return jnp.minimum(x, 0.0) - jnp.log1p(jnp.exp(-jnp.abs(x)))


def _const_spec(shape):
    nd = len(shape)
    return pl.BlockSpec(shape, lambda *_: (0,) * nd, pipeline_mode=pl.Buffered(1))


def _params(sem):
    return pltpu.CompilerParams(dimension_semantics=sem, vmem_limit_bytes=VMEM_LIMIT)


def _ple_tail(x1, p, gw_ref, pw_ref):
    gate = _sigmoid(_wdot(x1.astype(BF16), gw_ref))
    return x1 + gate * _wdot(p.astype(BF16), pw_ref)


def _conv_layer_kernel(*refs, nb, t, time_chunk, lane_chunk, has_prev):
    if has_prev:
        (x_ref, p_ref, prev_ref, g_ref, win_ref, cw_ref, cb_ref, lng_ref, lnb_ref, wout_ref,
         gw_ref, pw_ref, y_ref, st_ref, xs_ref, ps_ref, ys_ref, uext_ref, c_ref, zs_ref) = refs
    else:
        (x_ref, p_ref, g_ref, win_ref, cw_ref, cb_ref, lng_ref, lnb_ref, wout_ref,
         gw_ref, pw_ref, y_ref, st_ref, xs_ref, ps_ref, ys_ref, uext_ref, c_ref, zs_ref) = refs
    e = cw_ref.shape[-1]

    for tt in range(t):
        xs_ref[pl.ds(tt * nb, nb), :] = x_ref[:, tt, :]
        ps_ref[pl.ds(tt * nb, nb), :] = p_ref[:, tt, :]

    if has_prev:
        uext_ref[pl.ds(HALO_PAD - HALO, HALO)] = prev_ref[...]
    else:
        s = pl.program_id(1)

        @pl.when(s == 0)
        def _():
            uext_ref[pl.ds(0, HALO_PAD)] = jnp.zeros((HALO_PAD, nb, e), F32)

        @pl.when(s > 0)
        def _():
            uext_ref[pl.ds(0, HALO_PAD)] = uext_ref[pl.ds(t, HALO_PAD)]

    h = _rms(xs_ref[...], g_ref[...]).astype(BF16)
    for l0 in range(0, e, lane_chunk):
        lanes = pl.ds(l0, lane_chunk)
        a = _wdot(h, win_ref, l0, lane_chunk)
        b = _wdot(h, win_ref, e + l0, lane_chunk)
        uext_ref[pl.ds(HALO_PAD, t), :, lanes] = (a * _sigmoid(b)).reshape(t, nb, lane_chunk)
        for t0 in range(0, t, time_chunk):
            for b0 in range(0, nb, SUBLANES):
                acc = jnp.broadcast_to(cb_ref[:, lanes], (time_chunk, SUBLANES, lane_chunk))
                for k in range(CONV_W):
                    win = uext_ref[pl.ds(HALO_PAD - HALO + t0 + k, time_chunk), pl.ds(b0, SUBLANES), lanes]
                    acc = acc + win * cw_ref[k, :, lanes]
                c_ref[pl.ds(t0, time_chunk), pl.ds(b0, SUBLANES), lanes] = acc
        zs_ref[:, lanes] = _silu(_wdot(h, win_ref, 2 * e + l0, lane_chunk))

    if has_prev:
        st_ref[...] = uext_ref[pl.ds(HALO_PAD + t - HALO, HALO)]
    else:
        @pl.when(pl.program_id(1) == pl.num_programs(1) - 1)
        def _():
            st_ref[...] = uext_ref[pl.ds(HALO_PAD + t - HALO, HALO)]

    th = t // 2
    for half in range(2):
        rows = pl.ds(half * th * nb, th * nb)
        c = c_ref[pl.ds(half * th, th)].reshape(th * nb, e)
        mu = jnp.mean(c, axis=-1, keepdims=True)
        cc = c - mu
        ln = cc * lax.rsqrt(jnp.mean(cc * cc, axis=-1, keepdims=True) + EPS) * lng_ref[...] + lnb_ref[...]
        gated = (_silu(ln) * zs_ref[rows, :]).astype(BF16)
        x1 = xs_ref[rows, :] + _wdot(gated, wout_ref)
        ys_ref[rows, :] = _ple_tail(x1, ps_ref[rows, :], gw_ref, pw_ref)
    for tt in range(t):
        y_ref[:, tt, :] = ys_ref[pl.ds(tt * nb, nb), :]


def _conv_layer(x, p_all, layer, prev, g, w_in, conv_w, conv_b, ln_g, ln_b, w_out, gate_w, ple_w, *, tm):
    bsz, s, d = x.shape
    e = conv_w.shape[1]
    pd = p_all.shape[-1]
    has_prev = prev is not None
    conv_w8 = jnp.broadcast_to(conv_w[:, None, :], (CONV_W, SUBLANES, e))
    conv_b8 = jnp.broadcast_to(conv_b, (SUBLANES, e))
    weights = (g, w_in, conv_w8, conv_b8, ln_g, ln_b, w_out, gate_w, ple_w)
    if has_prev:
        nb, t = bsz, s
    else:
        nb, t = SUBLANES, tm // SUBLANES
        assert s % t == 0
    time_chunk = min(t, CONV_TIME_CHUNK)
    assert bsz % nb == 0 and nb % SUBLANES == 0 and t % time_chunk == 0 and t % 2 == 0
    x_spec = pl.BlockSpec((nb, t, d), lambda gb, i: (gb, i, 0))
    st_spec = pl.BlockSpec((HALO, nb, e), lambda gb, i: (0, gb, 0))
    in_specs = [x_spec, pl.BlockSpec((None, nb, t, pd), lambda gb, i: (layer, gb, i, 0))]
    args = (x, p_all)
    if has_prev:
        in_specs.append(pl.BlockSpec((None, HALO, nb, e), lambda gb, i: (layer, 0, gb, 0)))
        args += (prev,)
    kern = functools.partial(_conv_layer_kernel, nb=nb, t=t, time_chunk=time_chunk,
                             lane_chunk=CONV_LANE_CHUNK, has_prev=has_prev)
    m = nb * t
    return pl.pallas_call(
        kern,
        grid=(bsz // nb, s // t),
        in_specs=in_specs + [_const_spec(w.shape) for w in weights],
        out_specs=[x_spec, st_spec],
        out_shape=[jax.ShapeDtypeStruct((bsz, s, d), F32), jax.ShapeDtypeStruct((HALO, bsz, e), F32)],
        scratch_shapes=[pltpu.VMEM((m, d), F32), pltpu.VMEM((m, pd), F32), pltpu.VMEM((m, d), F32),
                        pltpu.VMEM((HALO_PAD + t, nb, e), F32), pltpu.VMEM((t, nb, e), F32),
                        pltpu.VMEM((m, e), F32)],
        compiler_params=_params(("arbitrary", "arbitrary")),
        name="conv_layer_decode" if has_prev else "conv_layer_prompt",
    )(*args, *weights)


def _head_norm(raw, bd_ref, gain):
    sq = (raw * raw).astype(BF16)
    ms = jnp.concatenate([_wdot(sq[:, c0:c0 + MXU_DEPTH], bd_ref) for c0 in range(0, raw.shape[1], MXU_DEPTH)],
                         axis=1) * (1.0 / HEAD_DIM)
    return raw * lax.rsqrt(ms + EPS) * gain


def _head_tiles(arr, n_heads, lane, extra):
    for hd in range(n_heads):
        pair = arr[:, (hd // 2) * LANES:(hd // 2 + 1) * LANES]
        if hd % 2 == 1:
            pair = pltpu.roll(pair, HEAD_DIM, axis=1)
        yield hd, jnp.where(lane < HEAD_DIM, pair, extra(hd))


def _bias_lanes(c, lane, piece_lane, ones_lane):
    pieces = [p.astype(F32) for p in _split3(c)]
    base = jnp.where((lane >= ones_lane) & (lane < ones_lane + 3), 1.0, 0.0)

    def extra(hd):
        tile = base
        for i, piece in enumerate(pieces):
            tile = jnp.where(lane == piece_lane + i, pltpu.roll(piece, (piece_lane + i - hd) % LANES, axis=1), tile)
        return tile

    return extra


def _kv_kernel(*refs, n_heads, prompt, n_parts):
    if prompt:
        (x_ref, g_ref, wkv_ref, wf_ref, fb_ref, kg_ref, bd_ref,
         k_ref, v_ref, lf_ref, c_ref, kp_ref, vp_ref, carry_ref) = refs

        @pl.when(pl.program_id(1) == 0)
        def _():
            carry_ref[...] = jnp.zeros(carry_ref.shape, F32)
    else:
        (x_ref, g_ref, wkv_ref, wf_ref, fb_ref, kg_ref, bd_ref, k_ref, v_ref, lf_ref) = refs
    e_b = n_heads * HEAD_DIM
    m = x_ref.shape[-2] // n_parts
    row = lax.broadcasted_iota(jnp.int32, (m, m), 0)
    col = lax.broadcasted_iota(jnp.int32, (m, m), 1)
    tri = jnp.where(col <= row, 1.0, 0.0).astype(BF16)
    lane = lax.broadcasted_iota(jnp.int32, (m, LANES), 1)
    ones_col = jnp.where(lane == HEAD_DIM, 1.0, 0.0)
    for part in range(n_parts):
        rows = pl.ds(part * m, m)
        h = _rms(x_ref[0, rows, :], g_ref[...]).astype(BF16)
        kn = _head_norm(_wdot(h, wkv_ref, 0, e_b), bd_ref, kg_ref[...])
        v = _wdot(h, wkv_ref, e_b, e_b)
        logf = _log_sigmoid(_wdot(h, wf_ref) + fb_ref[...])
        k_ref[0, rows, :] = kn
        v_ref[0, rows, :] = v
        lf_ref[0, rows, :] = logf[:, :n_heads]
        if not prompt:
            continue
        c = carry_ref[...]
        for piece in _split3(logf * LOG2E):
            c = c + _dot(tri, piece)
        carry_ref[...] = c[m - 1:m, :]
        c_ref[0, rows, :] = c
        for hd, tile in _head_tiles(kn, n_heads, lane, _bias_lanes(-c, lane, HEAD_DIM, HEAD_DIM + 3)):
            kp_ref[0, hd, rows, :] = tile.astype(BF16)
        for hd, tile in _head_tiles(v, n_heads, lane, lambda hd: ones_col):
            vp_ref[0, hd, rows, :] = tile.astype(BF16)


def _block_diag_ones():
    idx = np.arange(MXU_DEPTH) // HEAD_DIM
    return _lane_tiles(jnp.asarray((idx[:, None] == idx[None, :]).astype(np.float32)))


def _kv_proj(x, g, w_kv, w_f, f_bias, k_gain, bd, *, n_heads, prompt, tm):
    bsz, s, d = x.shape
    e_b = n_heads * HEAD_DIM
    outs = [jax.ShapeDtypeStruct((bsz, s, e_b), F32), jax.ShapeDtypeStruct((bsz, s, e_b), F32),
            jax.ShapeDtypeStruct((bsz, s, n_heads), F32)]
    weights = [g, w_kv, w_f, f_bias, k_gain, bd]
    if prompt:
        assert s % tm == 0
        grid = (bsz, s // tm)
        tok = lambda width: pl.BlockSpec((1, tm, width), lambda b, i: (b, i, 0))
        head_spec = pl.BlockSpec((1, n_heads, tm, LANES), lambda b, i: (b, 0, i, 0))
        out_specs = [tok(e_b), tok(e_b), tok(n_heads), tok(LANES), head_spec, head_spec]
        outs += [jax.ShapeDtypeStruct((bsz, s, LANES), F32),
                 jax.ShapeDtypeStruct((bsz, n_heads, s, LANES), BF16),
                 jax.ShapeDtypeStruct((bsz, n_heads, s, LANES), BF16)]
        x_spec = tok(d)
        scratch = [pltpu.VMEM((1, LANES), F32)]
        sem = ("arbitrary", "arbitrary")
    else:
        grid = (1,)
        x2 = x.reshape(1, bsz * s, d)
        full = lambda width: pl.BlockSpec((1, bsz * s, width), lambda i: (0, 0, 0))
        outs = [jax.ShapeDtypeStruct((1, bsz * s, o.shape[-1]), F32) for o in outs]
        out_specs = [full(e_b), full(e_b), full(n_heads)]
        x_spec = full(d)
        x = x2
        scratch = []
        sem = ("arbitrary",)
    res = pl.pallas_call(
        functools.partial(_kv_kernel, n_heads=n_heads, prompt=prompt, n_parts=KV_PARTS if prompt else 1),
        grid=grid,
        in_specs=[x_spec] + [_const_spec(w.shape) for w in weights],
        out_specs=out_specs,
        out_shape=outs,
        scratch_shapes=scratch,
        compiler_params=_params(sem),
        name="kv_proj_prompt" if prompt else "kv_proj_decode",
    )(x, *weights)
    if not prompt:
        res = [r.reshape(bsz, s, r.shape[-1]) for r in res]
    return res


def _fox_q_kernel(*refs, n_heads, prompt, n_parts):
    if prompt:
        x_ref, c_ref, g_ref, wq_ref, qg_ref, bd_ref, qp_ref, zs_ref = refs
    else:
        x_ref, g_ref, wq_ref, qg_ref, bd_ref, qn_ref, zs_ref = refs
    e_b = n_heads * HEAD_DIM
    m = x_ref.shape[-2] // n_parts
    lane = lax.broadcasted_iota(jnp.int32, (m, LANES), 1)
    for part in range(n_parts):
        rows = pl.ds(part * m, m)
        h = _rms(x_ref[0, rows, :], g_ref[...]).astype(BF16)
        qn = _head_norm(_wdot(h, wq_ref, 0, e_b), bd_ref, qg_ref[...])
        zs_ref[0, rows, :] = _silu(_wdot(h, wq_ref, e_b, e_b))
        if not prompt:
            qn_ref[0, rows, :] = qn
            continue
        extra = _bias_lanes(c_ref[0, rows, :], lane, HEAD_DIM + 3, HEAD_DIM)
        for hd, tile in _head_tiles(qn, n_heads, lane, extra):
            qp_ref[0, hd, rows, :] = tile.astype(BF16)


def _fox_q(x, c, g, w_q, q_gain, bd, *, n_heads, prompt, tm):
    bsz, s, d = x.shape
    e_b = n_heads * HEAD_DIM
    if prompt:
        grid = (bsz, s // tm)
        tok = lambda width: pl.BlockSpec((1, tm, width), lambda b, i: (b, i, 0))
        weights = [g, w_q, q_gain, bd]
        args = (x, c)
        in_specs = [tok(d), tok(LANES)]
        out_specs = [pl.BlockSpec((1, n_heads, tm, LANES), lambda b, i: (b, 0, i, 0)), tok(e_b)]
        outs = [jax.ShapeDtypeStruct((bsz, n_heads, s, LANES), BF16), jax.ShapeDtypeStruct((bsz, s, e_b), F32)]
        sem = ("arbitrary", "arbitrary")
    else:
        grid = (1,)
        full = lambda width: pl.BlockSpec((1, bsz * s, width), lambda i: (0, 0, 0))
        weights = [g, w_q, q_gain, bd]
        args = (x.reshape(1, bsz * s, d),)
        in_specs = [full(d)]
        out_specs = [full(e_b), full(e_b)]
        outs = [jax.ShapeDtypeStruct((1, bsz * s, e_b), F32)] * 2
        sem = ("arbitrary",)
    res = pl.pallas_call(
        functools.partial(_fox_q_kernel, n_heads=n_heads, prompt=prompt, n_parts=KV_PARTS if prompt else 1),
        grid=grid,
        in_specs=in_specs + [_const_spec(w.shape) for w in weights],
        out_specs=out_specs,
        out_shape=outs,
        compiler_params=_params(sem),
        name="fox_q_prompt" if prompt else "fox_q_decode",
    )(*args, *weights)
    if not prompt:
        res = [r.reshape(bsz, s, e_b) for r in res]
    return res


def _attn_prompt_kernel(q_ref, k_ref, v_ref, o_ref, *, tq):
    n_h, s_len = q_ref.shape[1], q_ref.shape[2]
    th = tq // 2
    lane = lax.broadcasted_iota(jnp.int32, (tq, LANES), 1)
    tri_full = (lax.broadcasted_iota(jnp.int32, (tq, th), 1) <= lax.broadcasted_iota(jnp.int32, (tq, th), 0))
    tri_half = tri_full[:th]

    def fold(q, hh, k0, nk, m_old, acc, mask):
        s = _dot_nt(q, k_ref[0, hh, pl.ds(k0, nk), :])
        if mask is not None:
            s = jnp.where(mask, s, NEG_BIG)
        m_new = jnp.maximum(m_old, jnp.max(s, axis=-1, keepdims=True))
        p = jnp.exp2(s - m_new).astype(BF16)
        pv = _dot(p, v_ref[0, hh, pl.ds(k0, nk), :])
        return m_new, jnp.exp2(m_old - m_new) * acc + pv

    for qi in range(s_len // tq):
        q0 = qi * tq
        outs = []
        state = [(jnp.full((tq, 1), NEG_BIG, F32), jnp.zeros((tq, LANES), F32)) for _ in range(n_h)]
        for j in range(qi):
            for hh in range(n_h):
                state[hh] = fold(q_ref[0, hh, pl.ds(q0, tq), :], hh, j * tq, tq, *state[hh], None)
        for hh in range(n_h):
            m, acc = fold(q_ref[0, hh, pl.ds(q0, tq), :], hh, q0, th, *state[hh], tri_full)
            m_lo, acc_lo = fold(q_ref[0, hh, pl.ds(q0 + th, th), :], hh, q0 + th, th, m[th:], acc[th:], tri_half)
            acc = jnp.concatenate([acc[:th], acc_lo], axis=0)
            denom = jnp.sum(jnp.where(lane == HEAD_DIM, acc, 0.0), axis=-1, keepdims=True)
            outs.append(acc / denom)
        for pair in range(n_h // 2):
            o_ref[0, pl.ds(q0, tq), pl.ds(pair * LANES, LANES)] = jnp.where(
                lane < HEAD_DIM, outs[2 * pair], pltpu.roll(outs[2 * pair + 1], HEAD_DIM, axis=1))


def _attn_prompt(qp, kp, vp, *, tq, heads_per_step):
    bsz, n_heads, s, _ = qp.shape
    hps = heads_per_step
    assert s % tq == 0 and n_heads % hps == 0 and hps % 2 == 0 and tq % (2 * SUBLANES) == 0
    head_spec = pl.BlockSpec((1, hps, s, LANES), lambda b, hp: (b, hp, 0, 0))
    return pl.pallas_call(
        functools.partial(_attn_prompt_kernel, tq=tq),
        grid=(bsz, n_heads // hps),
        in_specs=[head_spec, head_spec, head_spec],
        out_specs=pl.BlockSpec((1, s, hps * HEAD_DIM), lambda b, hp: (b, 0, hp)),
        out_shape=jax.ShapeDtypeStruct((bsz, s, n_heads * HEAD_DIM), F32),
        compiler_params=_params(("arbitrary", "arbitrary")),
        name="attn_prompt",
    )(qp, kp, vp)


def _attn_decode_kernel(*refs, n_pages_step, n_heads, t, group_heads):
    g = n_pages_step
    qn_ref, kn_ref, vn_ref, lfn_ref, exp_ref, rep_ref = refs[1:7]
    k_refs = refs[7:7 + g]
    v_refs = refs[7 + g:7 + 2 * g]
    lf_refs = refs[7 + 2 * g:7 + 3 * g]
    o_ref = refs[7 + 3 * g]
    qexp_ref, m_ref, l_ref, acc_ref, carry_ref = refs[8 + 3 * g:]
    ht = n_heads * t
    e_b = n_heads * HEAD_DIM
    page = k_refs[0].shape[1]
    s_idx = pl.program_id(1)

    @pl.when(s_idx == 0)
    def _():
        spread = _dot(rep_ref[...], qn_ref[0].astype(BF16).astype(F32))
        keep = (lax.broadcasted_iota(jnp.int32, (ht, e_b), 0) // t
                == lax.broadcasted_iota(jnp.int32, (ht, e_b), 1) // HEAD_DIM)
        qexp_ref[...] = jnp.where(keep, spread, 0.0).astype(BF16)
        m_ref[...] = jnp.full(m_ref.shape, NEG_BIG, F32)
        l_ref[...] = jnp.zeros(l_ref.shape, F32)
        acc_ref[...] = jnp.zeros(acc_ref.shape, F32)
        carry_ref[...] = jnp.zeros(carry_ref.shape, F32)

    row = lax.broadcasted_iota(jnp.int32, (page, page), 0)
    col = lax.broadcasted_iota(jnp.int32, (page, page), 1)
    upper = jnp.where(row <= col, 1.0, 0.0).astype(BF16)

    def rows_per_head(a):
        return jnp.broadcast_to(a[:, None, :], (n_heads, t, a.shape[-1])).reshape(ht, a.shape[-1])

    def cumsum_keys(c, lf):
        for piece in _split3(lf):
            c = c + _dot(piece, upper)
        return c

    def page_bias(lf_t):
        lf2 = lf_t * LOG2E
        before = carry_ref[...]
        carry_ref[...] = before + jnp.sum(lf2, axis=1, keepdims=True)
        return cumsum_keys(jnp.broadcast_to(before, (n_heads, page)), lf2)

    gr, gc = group_heads * t, group_heads * HEAD_DIM
    n_groups = n_heads // group_heads
    rows = lambda a, c: a[c * gr:(c + 1) * gr]
    q_group = lambda c: qexp_ref[pl.ds(c * gr, gr), pl.ds(c * gc, gc)]
    slab = lambda ref, c: ref[pl.ds(c * gc, gc), :].astype(BF16)
    by_group = lambda f: jnp.concatenate([f(c) for c in range(n_groups)], axis=0)

    def update(s, pv_t):
        m_old = m_ref[...]
        m_new = jnp.maximum(m_old, jnp.max(s, axis=1, keepdims=True))
        p = jnp.exp2(s - m_new)
        alpha = jnp.exp2(m_old - m_new)
        m_ref[...] = m_new
        l_ref[...] = alpha * l_ref[...] + jnp.sum(p, axis=1, keepdims=True)
        alpha_row = jnp.broadcast_to(alpha, (ht, ht)).T[0:1, :]
        p = p.astype(BF16)
        for c in range(n_groups):
            acc_ref[c, :, pl.ds(0, gr)] = (alpha_row[:, c * gr:(c + 1) * gr] * acc_ref[c, :, pl.ds(0, gr)]
                                           + pv_t(p, c))

    s = jnp.concatenate(
        [by_group(lambda c: _dot(q_group(c), slab(k_refs[i], c))) - rows_per_head(page_bias(lf_refs[i][...]))
         for i in range(g)], axis=1)
    update(s, lambda p, c: sum(
        _dot_nt(jnp.concatenate([slab(v_refs[i], c), slab(v_refs[i + 1], c)], axis=1),
                rows(p, c)[:, i * page:(i + 2) * page]) for i in range(0, g, 2)))

    @pl.when(s_idx == pl.num_programs(1) - 1)
    def _():
        pad = lambda a: jnp.concatenate([a, jnp.zeros((page - t, a.shape[1]), F32)], axis=0)
        lexp = jnp.zeros((page, ht), F32)
        for piece in _split3(pad(lfn_ref[0]) * LOG2E):
            lexp = lexp + _dot(piece, exp_ref[...])
        c_new = cumsum_keys(rows_per_head(jnp.broadcast_to(carry_ref[...], (n_heads, page))), lexp.T)
        k_pad = pad(kn_ref[0]).astype(BF16)
        v_pad = pad(vn_ref[0]).astype(BF16)
        s_new = by_group(lambda c: _dot_nt(q_group(c), k_pad[:, c * gc:(c + 1) * gc])) - c_new
        key = lax.broadcasted_iota(jnp.int32, (ht, page), 1)
        tok = lax.broadcasted_iota(jnp.int32, (ht, page), 0) % t
        tall = lambda a: jnp.concatenate([a, jnp.zeros((ht - a.shape[0], a.shape[1]), F32)], axis=0)
        update(jnp.where(key <= tok, s_new, NEG_BIG),
               lambda p, c: tall(_dot(rows(p, c), v_pad[:, c * gc:(c + 1) * gc])).T[:, :gr])
        keep = (lax.broadcasted_iota(jnp.int32, (gr, gc), 0) // t
                == lax.broadcasted_iota(jnp.int32, (gr, gc), 1) // HEAD_DIM)
        outs = []
        for c in range(n_groups):
            o = acc_ref[c].T[:gr, :] / rows(l_ref[...], c)
            outs.append(jnp.sum(jnp.where(keep, o, 0.0).reshape(group_heads, t, gc), axis=0))
        o_ref[0] = jnp.concatenate(outs, axis=1)


def _attn_decode(qn, k_new, v_new, lf_new, cache_k, cache_v, cache_logf, page_table, *, n_heads, pages_per_step):
    bsz, t, e_b = qn.shape
    n_pool, page, _, _ = cache_k.shape
    n_pages = page_table.shape[1]
    g = pages_per_step
    ht = n_heads * t
    gh = MXU_DEPTH // HEAD_DIM
    assert n_pages % g == 0 and g % 2 == 0 and ht == page == LANES and t % SUBLANES == 0 and n_heads % gh == 0
    ck = jnp.transpose(cache_k, (0, 2, 3, 1)).reshape(n_pool, e_b, page)
    cv = jnp.transpose(cache_v, (0, 2, 3, 1)).reshape(n_pool, e_b, page)
    clf = jnp.transpose(cache_logf, (0, 2, 1))
    expand = jnp.asarray((np.arange(n_heads)[:, None] == (np.arange(ht)[None, :] // t)).astype(np.float32), BF16)
    repeat = jnp.asarray((np.arange(ht)[:, None] % t == np.arange(t)[None, :]).astype(np.float32))
    tok = lambda width: pl.BlockSpec((1, t, width), lambda b, s, pt: (b, 0, 0))
    page_spec = lambda rows, i: pl.BlockSpec((None, rows, page), lambda b, s, pt, i=i: (pt[b, s * g + i], 0, 0))
    in_specs = ([tok(e_b), tok(e_b), tok(e_b), tok(n_heads),
                 pl.BlockSpec(expand.shape, lambda b, s, pt: (0, 0)),
                 pl.BlockSpec(repeat.shape, lambda b, s, pt: (0, 0))]
                + [page_spec(e_b, i) for i in range(g)] * 2
                + [page_spec(n_heads, i) for i in range(g)])
    return pl.pallas_call(
        functools.partial(_attn_decode_kernel, n_pages_step=g, n_heads=n_heads, t=t, group_heads=gh),
        grid_spec=pltpu.PrefetchScalarGridSpec(
            num_scalar_prefetch=1,
            grid=(bsz, n_pages // g),
            in_specs=in_specs,
            out_specs=tok(e_b),
            scratch_shapes=[pltpu.VMEM((ht, e_b), BF16), pltpu.VMEM((ht, 1), F32), pltpu.VMEM((ht, 1), F32),
                            pltpu.VMEM((n_heads // gh, gh * HEAD_DIM, LANES), F32),
                            pltpu.VMEM((n_heads, 1), F32)]),
        out_shape=jax.ShapeDtypeStruct((bsz, t, e_b), F32),
        compiler_params=_params(("arbitrary", "arbitrary")),
        name="attn_decode",
    )(page_table, qn, k_new, v_new, lf_new, expand, repeat, *([ck] * g), *([cv] * g), *([clf] * g))


def _fox_out_kernel(o_ref, zs_ref, x_ref, p_ref, wout_ref, gw_ref, pw_ref, y_ref):
    m = x_ref.shape[0] * x_ref.shape[1]
    two_d = lambda r: r[...].reshape(m, r.shape[-1])
    gated = (two_d(o_ref) * two_d(zs_ref)).astype(BF16)
    x1 = two_d(x_ref) + _wdot(gated, wout_ref)
    y_ref[...] = _ple_tail(x1, two_d(p_ref), gw_ref, pw_ref).reshape(y_ref.shape)


def _fox_out(o, zs, x, p_all, layer, w_out, gate_w, ple_w, *, tm):
    bsz, s, d = x.shape
    pd = p_all.shape[-1]
    if s % tm == 0:
        nb, ns = 1, tm
    else:
        nb, ns = bsz, s
    tok = lambda width: pl.BlockSpec((nb, ns, width), lambda b, i: (b, i, 0))
    weights = (w_out, gate_w, ple_w)
    return pl.pallas_call(
        _fox_out_kernel,
        grid=(bsz // nb, s // ns),
        in_specs=[tok(o.shape[-1]), tok(zs.shape[-1]), tok(d),
                  pl.BlockSpec((None, nb, ns, pd), lambda b, i: (layer, b, i, 0))]
                 + [_const_spec(w.shape) for w in weights],
        out_specs=tok(d),
        out_shape=jax.ShapeDtypeStruct((bsz, s, d), F32),
        compiler_params=_params(("arbitrary", "arbitrary")),
        name="fox_out",
    )(o, zs, x, p_all, *weights)


TM_PROMPT = 256
TM_KV = 512
KV_PARTS = 4
TQ_PROMPT = 512
ATTN_HEADS_PER_STEP = 2
PAGES_PER_STEP = 16


def kernel(x_prompt, x_sample, state_conv, cache_k, cache_v, cache_logf, page_table, p_prompt, p_sample,
           a_norm, a_w_in, a_conv_w, a_conv_b, a_ln_g, a_ln_b, a_w_out, kv_norm, kv_w, kv_k_norm, kv_f_bias,
           b_norm, b_w_q, b_q_norm, b_w_out, ple_w, ple_gate_w):
    n_a = a_norm.shape[0]
    n_b = b_norm.shape[0]
    n_heads = kv_f_bias.shape[0]
    e_b = n_heads * HEAD_DIM
    d = x_prompt.shape[-1]
    row = lambda v: v.reshape(1, -1)
    bf = _lane_tiles

    a_w_in16, a_w_out16, b_w_q16, b_w_out16 = bf(a_w_in), bf(a_w_out), bf(b_w_q), bf(b_w_out)
    ple_w16, gate_w16 = bf(ple_w), bf(ple_gate_w)
    w_kv16 = bf(kv_w[:, :2 * e_b])
    w_f16 = bf(jnp.pad(kv_w[:, 2 * e_b:], ((0, 0), (0, LANES - n_heads))))
    f_bias = jnp.pad(row(kv_f_bias), ((0, 0), (0, LANES - n_heads)))
    k_gain = jnp.tile(row(kv_k_norm), (1, n_heads))
    assert e_b % MXU_DEPTH == 0 and MXU_DEPTH % HEAD_DIM == 0
    bd = _block_diag_ones()
    tm = min(TM_PROMPT, x_prompt.shape[1])
    tq = min(TQ_PROMPT, x_prompt.shape[1])

    def trunk(x, p_all, prev, prompt):
        conv_states = []
        if prev is not None:
            prev = jnp.transpose(prev, (0, 2, 1, 3))
        for i in range(n_a):
            x, st = _conv_layer(x, p_all, i, prev, row(a_norm[i]), a_w_in16[i],
                                a_conv_w[i], row(a_conv_b[i]), row(a_ln_g[i]), row(a_ln_b[i]), a_w_out16[i],
                                gate_w16[i], ple_w16[i], tm=tm)
            conv_states.append(st)
        kv = _kv_proj(x, row(kv_norm), w_kv16, w_f16, f_bias, k_gain, bd, n_heads=n_heads, prompt=prompt,
                      tm=min(TM_KV, x.shape[1]))
        k_new, v_new, lf_new = kv[:3]
        for j in range(n_b):
            q_gain = jnp.tile(row(b_q_norm[j]), (1, n_heads)) * (LOG2E * HEAD_DIM ** -0.5)
            if prompt:
                qp, zs = _fox_q(x, kv[3], row(b_norm[j]), b_w_q16[j], q_gain, bd, n_heads=n_heads, prompt=True,
                                tm=min(TM_KV, x.shape[1]))
                o = _attn_prompt(qp, kv[4], kv[5], tq=tq, heads_per_step=ATTN_HEADS_PER_STEP)
            else:
                qn, zs = _fox_q(x, None, row(b_norm[j]), b_w_q16[j], q_gain, bd, n_heads=n_heads, prompt=False, tm=tm)
                o = _attn_decode(qn, k_new, v_new, lf_new, cache_k, cache_v, cache_logf, page_table,
                                 n_heads=n_heads, pages_per_step=min(PAGES_PER_STEP, page_table.shape[1]))
            x = _fox_out(o, zs, x, p_all, n_a + j, b_w_out16[j], gate_w16[n_a + j], ple_w16[n_a + j], tm=tm)
        shape4 = k_new.shape[:2] + (n_heads, HEAD_DIM)
        conv_state = jnp.transpose(jnp.stack(conv_states, axis=0), (0, 2, 1, 3))
        return x, conv_state, k_new.reshape(shape4), v_new.reshape(shape4), lf_new

    y_p, conv_p, k_p, v_p, lf_p = trunk(x_prompt, p_prompt, None, True)
    y_s, conv_s, k_s, v_s, lf_s = trunk(x_sample, p_sample, state_conv, False)
    return (y_p, y_s, conv_p, k_p, v_p, lf_p, conv_s, k_s, v_s, lf_s)
```

```python
import functools
import math

import numpy as np
import jax
import jax.numpy as jnp
from jax import lax
from jax.experimental import pallas as pl
from jax.experimental.pallas import tpu as pltpu

F32 = jnp.float32
BF16 = jnp.bfloat16

EPS = 1e-6
CONV_W = 31
HALO = CONV_W - 1
HEAD_DIM = 64
LANES = 128
SUBLANES = 8
MXU_DEPTH = 256
HALO_PAD = 32
CONV_TIME_CHUNK = 8
CONV_LANE_CHUNK = 2 * LANES
LOG2E = math.log2(math.e)
NEG_BIG = -1e30
VMEM_LIMIT = 56 * 1024 * 1024


def _dot(a, b):
    return jnp.dot(a, b, preferred_element_type=F32)


def _dot_nt(a, b):
    return lax.dot_general(a, b, (((1,), (1,)), ((), ())), preferred_element_type=F32)


def _lane_tiles(w):
    *lead, k, n = w.shape
    return jnp.moveaxis(w.astype(BF16).reshape(*lead, k, n // LANES, LANES), -2, -3)


def _wdot(a, w_ref, col0=0, ncols=None, lead=()):
    first = col0 // LANES
    count = w_ref.shape[-3] - first if ncols is None else ncols // LANES
    tiles = [w_ref[lead + (j,)] for j in range(first, first + count)]
    return _dot(a, tiles[0] if count == 1 else jnp.concatenate(tiles, axis=1))


def _split3(x):
    p0 = x.astype(BF16)
    r = x - p0.astype(F32)
    p1 = r.astype(BF16)
    r = r - p1.astype(F32)
    return p0, p1, r.astype(BF16)


def _split2(x):
    p0 = x.astype(BF16)
    return p0, (x - p0.astype(F32)).astype(BF16)


def _rms(x, g):
    return x * lax.rsqrt(jnp.mean(x * x, axis=-1, keepdims=True) + EPS) * g


def _sigmoid(x):
    return 1.0 / (1.0 + jnp.exp(-x))


def _silu(x):
    return x * _sigmoid(x)


def _log_sigmoid(x):
    return jnp.minimum(x, 0.0) - jnp.log1p(jnp.exp(-jnp.abs(x)))


def _const_spec(shape):
    nd = len(shape)
    return pl.BlockSpec(shape, lambda *_: (0,) * nd, pipeline_mode=pl.Buffered(1))


def _params(sem):
    return pltpu.CompilerParams(dimension_semantics=sem, vmem_limit_bytes=VMEM_LIMIT)


def _ple_tail(x1, p, gw_ref, pw_ref):
    gate = _sigmoid(_wdot(x1.astype(BF16), gw_ref))
    return x1 + gate * _wdot(p.astype(BF16), pw_ref)


def _conv_layer_kernel(*refs, nb, t, time_chunk, lane_chunk, has_prev):
    if has_prev:
        (x_ref, p_ref, prev_ref, g_ref, win_ref, cw_ref, cb_ref, lng_ref, lnb_ref, wout_ref,
         gw_ref, pw_ref, y_ref, st_ref, xs_ref, ps_ref, ys_ref, uext_ref, c_ref, zs_ref) = refs
    else:
        (x_ref, p_ref, g_ref, win_ref, cw_ref, cb_ref, lng_ref, lnb_ref, wout_ref,
         gw_ref, pw_ref, y_ref, st_ref, xs_ref, ps_ref, ys_ref, uext_ref, c_ref, zs_ref) = refs
    e = cw_ref.shape[-1]

    for tt in range(t):
        xs_ref[pl.ds(tt * nb, nb), :] = x_ref[:, tt, :]
        ps_ref[pl.ds(tt * nb, nb), :] = p_ref[:, tt, :]

    if has_prev:
        uext_ref[pl.ds(HALO_PAD - HALO, HALO)] = prev_ref[...]
    else:
        s = pl.program_id(1)

        @pl.when(s == 0)
        def _():
            uext_ref[pl.ds(0, HALO_PAD)] = jnp.zeros((HALO_PAD, nb, e), F32)

        @pl.when(s > 0)
        def _():
            uext_ref[pl.ds(0, HALO_PAD)] = uext_ref[pl.ds(t, HALO_PAD)]

    h = _rms(xs_ref[...], g_ref[...]).astype(BF16)
    for l0 in range(0, e, lane_chunk):
        lanes = pl.ds(l0, lane_chunk)
        a = _wdot(h, win_ref, l0, lane_chunk)
        b = _wdot(h, win_ref, e + l0, lane_chunk)
        uext_ref[pl.ds(HALO_PAD, t), :, lanes] = (a * _sigmoid(b)).reshape(t, nb, lane_chunk)
        for t0 in range(0, t, time_chunk):
            for b0 in range(0, nb, SUBLANES):
                acc = jnp.broadcast_to(cb_ref[:, lanes], (time_chunk, SUBLANES, lane_chunk))
                for k in range(CONV_W):
                    win = uext_ref[pl.ds(HALO_PAD - HALO + t0 + k, time_chunk), pl.ds(b0, SUBLANES), lanes]
                    acc = acc + win * cw_ref[k, :, lanes]
                c_ref[pl.ds(t0, time_chunk), pl.ds(b0, SUBLANES), lanes] = acc
        zs_ref[:, lanes] = _silu(_wdot(h, win_ref, 2 * e + l0, lane_chunk))

    if has_prev:
        st_ref[...] = uext_ref[pl.ds(HALO_PAD + t - HALO, HALO)]
    else:
        @pl.when(pl.program_id(1) == pl.num_programs(1) - 1)
        def _():
            st_ref[...] = uext_ref[pl.ds(HALO_PAD + t - HALO, HALO)]

    th = t // 2
    for half in range(2):
        rows = pl.ds(half * th * nb, th * nb)
        c = c_ref[pl.ds(half * th, th)].reshape(th * nb, e)
        mu = jnp.mean(c, axis=-1, keepdims=True)
        cc = c - mu
        ln = cc * lax.rsqrt(jnp.mean(cc * cc, axis=-1, keepdims=True) + EPS) * lng_ref[...] + lnb_ref[...]
        gated = (_silu(ln) * zs_ref[rows, :]).astype(BF16)
        x1 = xs_ref[rows, :] + _wdot(gated, wout_ref)
        ys_ref[rows, :] = _ple_tail(x1, ps_ref[rows, :], gw_ref, pw_ref)
    for tt in range(t):
        y_ref[:, tt, :] = ys_ref[pl.ds(tt * nb, nb), :]


def _conv_layer(x, p_all, layer, prev, g, w_in, conv_w, conv_b, ln_g, ln_b, w_out, gate_w, ple_w, *, tm):
    bsz, s, d = x.shape
    e = conv_w.shape[1]
    pd = p_all.shape[-1]
    has_prev = prev is not None
    conv_w8 = jnp.broadcast_to(conv_w[:, None, :], (CONV_W, SUBLANES, e))
    conv_b8 = jnp.broadcast_to(conv_b, (SUBLANES, e))
    weights = (g, w_in, conv_w8, conv_b8, ln_g, ln_b, w_out, gate_w, ple_w)
    if has_prev:
        nb, t = bsz, s
    else:
        nb, t = SUBLANES, tm // SUBLANES
        assert s % t == 0
    time_chunk = min(t, CONV_TIME_CHUNK)
    assert bsz % nb == 0 and nb % SUBLANES == 0 and t % time_chunk == 0 and t % 2 == 0
    x_spec = pl.BlockSpec((nb, t, d), lambda gb, i: (gb, i, 0))
    st_spec = pl.BlockSpec((HALO, nb, e), lambda gb, i: (0, gb, 0))
    in_specs = [x_spec, pl.BlockSpec((None, nb, t, pd), lambda gb, i: (layer, gb, i, 0))]
    args = (x, p_all)
    if has_prev:
        in_specs.append(pl.BlockSpec((None, HALO, nb, e), lambda gb, i: (layer, 0, gb, 0)))
        args += (prev,)
    kern = functools.partial(_conv_layer_kernel, nb=nb, t=t, time_chunk=time_chunk,
                             lane_chunk=CONV_LANE_CHUNK, has_prev=has_prev)
    m = nb * t
    return pl.pallas_call(
        kern,
        grid=(bsz // nb, s // t),
        in_specs=in_specs + [_const_spec(w.shape) for w in weights],
        out_specs=[x_spec, st_spec],
        out_shape=[jax.ShapeDtypeStruct((bsz, s, d), F32), jax.ShapeDtypeStruct((HALO, bsz, e), F32)],
        scratch_shapes=[pltpu.VMEM((m, d), F32), pltpu.VMEM((m, pd), F32), pltpu.VMEM((m, d), F32),
                        pltpu.VMEM((HALO_PAD + t, nb, e), F32), pltpu.VMEM((t, nb, e), F32),
                        pltpu.VMEM((m, e), F32)],
        compiler_params=_params(("arbitrary", "arbitrary")),
        name="conv_layer_decode" if has_prev else "conv_layer_prompt",
    )(*args, *weights)


def _head_norm(raw, bd_ref, gain):
    sq = (raw * raw).astype(BF16)
    ms = jnp.concatenate([_wdot(sq[:, c0:c0 + MXU_DEPTH], bd_ref) for c0 in range(0, raw.shape[1], MXU_DEPTH)],
                         axis=1) * (1.0 / HEAD_DIM)
    return raw * lax.rsqrt(ms + EPS) * gain


def _head_tiles(arr, n_heads, lane, extra):
    for hd in range(n_heads):
        pair = arr[:, (hd // 2) * LANES:(hd // 2 + 1) * LANES]
        if hd % 2 == 1:
            pair = pltpu.roll(pair, HEAD_DIM, axis=1)
        yield hd, jnp.where(lane < HEAD_DIM, pair, extra(hd))


def _bias_lanes(c, lane, piece_lane, ones_lane):
    pieces = [p.astype(F32) for p in _split3(c)]
    base = jnp.where((lane >= ones_lane) & (lane < ones_lane + 3), 1.0, 0.0)

    def extra(hd):
        tile = base
        for i, piece in enumerate(pieces):
            tile = jnp.where(lane == piece_lane + i, pltpu.roll(piece, (piece_lane + i - hd) % LANES, axis=1), tile)
        return tile

    return extra


def _kv_kernel(*refs, n_heads, prompt, n_parts):
    if prompt:
        (x_ref, g_ref, wkv_ref, wf_ref, fb_ref, kg_ref, bd_ref,
         k_ref, v_ref, lf_ref, c_ref, kp_ref, vp_ref, carry_ref) = refs

        @pl.when(pl.program_id(1) == 0)
        def _():
            carry_ref[...] = jnp.zeros(carry_ref.shape, F32)
    else:
        (x_ref, g_ref, wkv_ref, wf_ref, fb_ref, kg_ref, bd_ref, k_ref, v_ref, lf_ref) = refs
    e_b = n_heads * HEAD_DIM
    m = x_ref.shape[-2] // n_parts
    row = lax.broadcasted_iota(jnp.int32, (m, m), 0)
    col = lax.broadcasted_iota(jnp.int32, (m, m), 1)
    tri = jnp.where(col <= row, 1.0, 0.0).astype(BF16)
    lane = lax.broadcasted_iota(jnp.int32, (m, LANES), 1)
    ones_col = jnp.where(lane == HEAD_DIM, 1.0, 0.0)
    for part in range(n_parts):
        rows = pl.ds(part * m, m)
        h = _rms(x_ref[0, rows, :], g_ref[...]).astype(BF16)
        kn = _head_norm(_wdot(h, wkv_ref, 0, e_b), bd_ref, kg_ref[...])
        v = _wdot(h, wkv_ref, e_b, e_b)
        logf = _log_sigmoid(_wdot(h, wf_ref) + fb_ref[...])
        k_ref[0, rows, :] = kn
        v_ref[0, rows, :] = v
        lf_ref[0, rows, :] = logf[:, :n_heads]
        if not prompt:
            continue
        c = carry_ref[...]
        for piece in _split3(logf * LOG2E):
            c = c + _dot(tri, piece)
        carry_ref[...] = c[m - 1:m, :]
        c_ref[0, rows, :] = c
        for hd, tile in _head_tiles(kn, n_heads, lane, _bias_lanes(-c, lane, HEAD_DIM, HEAD_DIM + 3)):
            kp_ref[0, hd, rows, :] = tile.astype(BF16)
        for hd, tile in _head_tiles(v, n_heads, lane, lambda hd: ones_col):
            vp_ref[0, hd, rows, :] = tile.astype(BF16)


def _block_diag_ones():
    idx = np.arange(MXU_DEPTH) // HEAD_DIM
    return _lane_tiles(jnp.asarray((idx[:, None] == idx[None, :]).astype(np.float32)))


def _kv_proj(x, g, w_kv, w_f, f_bias, k_gain, bd, *, n_heads, prompt, tm):
    bsz, s, d = x.shape
    e_b = n_heads * HEAD_DIM
    outs = [jax.ShapeDtypeStruct((bsz, s, e_b), F32), jax.ShapeDtypeStruct((bsz, s, e_b), F32),
            jax.ShapeDtypeStruct((bsz, s, n_heads), F32)]
    weights = [g, w_kv, w_f, f_bias, k_gain, bd]
    if prompt:
        assert s % tm == 0
        grid = (bsz, s // tm)
        tok = lambda width: pl.BlockSpec((1, tm, width), lambda b, i: (b, i, 0))
        head_spec = pl.BlockSpec((1, n_heads, tm, LANES), lambda b, i: (b, 0, i, 0))
        out_specs = [tok(e_b), tok(e_b), tok(n_heads), tok(LANES), head_spec, head_spec]
        outs += [jax.ShapeDtypeStruct((bsz, s, LANES), F32),
                 jax.ShapeDtypeStruct((bsz, n_heads, s, LANES), BF16),
                 jax.ShapeDtypeStruct((bsz, n_heads, s, LANES), BF16)]
        x_spec = tok(d)
        scratch = [pltpu.VMEM((1, LANES), F32)]
        sem = ("arbitrary", "arbitrary")
    else:
        grid = (1,)
        x2 = x.reshape(1, bsz * s, d)
        full = lambda width: pl.BlockSpec((1, bsz * s, width), lambda i: (0, 0, 0))
        outs = [jax.ShapeDtypeStruct((1, bsz * s, o.shape[-1]), F32) for o in outs]
        out_specs = [full(e_b), full(e_b), full(n_heads)]
        x_spec = full(d)
        x = x2
        scratch = []
        sem = ("arbitrary",)
    res = pl.pallas_call(
        functools.partial(_kv_kernel, n_heads=n_heads, prompt=prompt, n_parts=KV_PARTS if prompt else 1),
        grid=grid,
        in_specs=[x_spec] + [_const_spec(w.shape) for w in weights],
        out_specs=out_specs,
        out_shape=outs,
        scratch_shapes=scratch,
        compiler_params=_params(sem),
        name="kv_proj_prompt" if prompt else "kv_proj_decode",
    )(x, *weights)
    if not prompt:
        res = [r.reshape(bsz, s, r.shape[-1]) for r in res]
    return res


def _fox_q_kernel(*refs, n_heads, prompt, n_parts):
    if prompt:
        x_ref, c_ref, g_ref, wq_ref, qg_ref, bd_ref, qp_ref, zs_ref = refs
    else:
        x_ref, g_ref, wq_ref, qg_ref, bd_ref, qn_ref, zs_ref = refs
    e_b = n_heads * HEAD_DIM
    m = x_ref.shape[-2] // n_parts
    lane = lax.broadcasted_iota(jnp.int32, (m, LANES), 1)
    for part in range(n_parts):
        rows = pl.ds(part * m, m)
        h = _rms(x_ref[0, rows, :], g_ref[...]).astype(BF16)
        qn = _head_norm(_wdot(h, wq_ref, 0, e_b), bd_ref, qg_ref[...])
        zs_ref[0, rows, :] = _silu(_wdot(h, wq_ref, e_b, e_b)).astype(BF16)
        if not prompt:
            qn_ref[0, rows, :] = qn
            continue
        extra = _bias_lanes(c_ref[0, rows, :], lane, HEAD_DIM + 3, HEAD_DIM)
        for hd, tile in _head_tiles(qn, n_heads, lane, extra):
            qp_ref[0, hd, rows, :] = tile.astype(BF16)


def _fox_q(x, c, g, w_q, q_gain, bd, *, n_heads, prompt, tm):
    bsz, s, d = x.shape
    e_b = n_heads * HEAD_DIM
    if prompt:
        grid = (bsz, s // tm)
        tok = lambda width: pl.BlockSpec((1, tm, width), lambda b, i: (b, i, 0))
        weights = [g, w_q, q_gain, bd]
        args = (x, c)
        in_specs = [tok(d), tok(LANES)]
        out_specs = [pl.BlockSpec((1, n_heads, tm, LANES), lambda b, i: (b, 0, i, 0)), tok(e_b)]
        outs = [jax.ShapeDtypeStruct((bsz, n_heads, s, LANES), BF16), jax.ShapeDtypeStruct((bsz, s, e_b), BF16)]
        sem = ("arbitrary", "arbitrary")
    else:
        grid = (1,)
        full = lambda width: pl.BlockSpec((1, bsz * s, width), lambda i: (0, 0, 0))
        weights = [g, w_q, q_gain, bd]
        args = (x.reshape(1, bsz * s, d),)
        in_specs = [full(d)]
        out_specs = [full(e_b), full(e_b)]
        outs = [jax.ShapeDtypeStruct((1, bsz * s, e_b), F32), jax.ShapeDtypeStruct((1, bsz * s, e_b), BF16)]
        sem = ("arbitrary",)
    res = pl.pallas_call(
        functools.partial(_fox_q_kernel, n_heads=n_heads, prompt=prompt, n_parts=KV_PARTS if prompt else 1),
        grid=grid,
        in_specs=in_specs + [_const_spec(w.shape) for w in weights],
        out_specs=out_specs,
        out_shape=outs,
        compiler_params=_params(sem),
        name="fox_q_prompt" if prompt else "fox_q_decode",
    )(*args, *weights)
    if not prompt:
        res = [r.reshape(bsz, s, e_b) for r in res]
    return res


def _attn_prompt_kernel(q_ref, k_ref, v_ref, o_ref, *, tq):
    n_h, s_len = q_ref.shape[1], q_ref.shape[2]
    th = tq // 2
    lane = lax.broadcasted_iota(jnp.int32, (tq, LANES), 1)
    tri_full = (lax.broadcasted_iota(jnp.int32, (tq, th), 1) <= lax.broadcasted_iota(jnp.int32, (tq, th), 0))
    tri_half = tri_full[:th]

    def fold(q, hh, k0, nk, m_old, acc, mask):
        s = _dot_nt(q, k_ref[0, hh, pl.ds(k0, nk), :])
        if mask is not None:
            s = jnp.where(mask, s, NEG_BIG)
        m_new = jnp.maximum(m_old, jnp.max(s, axis=-1, keepdims=True))
        p = jnp.exp2(s - m_new).astype(BF16)
        pv = _dot(p, v_ref[0, hh, pl.ds(k0, nk), :])
        return m_new, jnp.exp2(m_old - m_new) * acc + pv

    for qi in range(s_len // tq):
        q0 = qi * tq
        outs = []
        state = [(jnp.full((tq, 1), NEG_BIG, F32), jnp.zeros((tq, LANES), F32)) for _ in range(n_h)]
        for j in range(qi):
            for hh in range(n_h):
                state[hh] = fold(q_ref[0, hh, pl.ds(q0, tq), :], hh, j * tq, tq, *state[hh], None)
        for hh in range(n_h):
            m, acc = fold(q_ref[0, hh, pl.ds(q0, tq), :], hh, q0, th, *state[hh], tri_full)
            m_lo, acc_lo = fold(q_ref[0, hh, pl.ds(q0 + th, th), :], hh, q0 + th, th, m[th:], acc[th:], tri_half)
            acc = jnp.concatenate([acc[:th], acc_lo], axis=0)
            denom = jnp.sum(jnp.where(lane == HEAD_DIM, acc, 0.0), axis=-1, keepdims=True)
            outs.append(acc / denom)
        for pair in range(n_h // 2):
            o_ref[0, pl.ds(q0, tq), pl.ds(pair * LANES, LANES)] = jnp.where(
                lane < HEAD_DIM, outs[2 * pair], pltpu.roll(outs[2 * pair + 1], HEAD_DIM, axis=1))


def _attn_prompt(qp, kp, vp, *, tq, heads_per_step):
    bsz, n_heads, s, _ = qp.shape
    hps = heads_per_step
    assert s % tq == 0 and n_heads % hps == 0 and hps % 2 == 0 and tq % (2 * SUBLANES) == 0
    head_spec = pl.BlockSpec((1, hps, s, LANES), lambda b, hp: (b, hp, 0, 0))
    return pl.pallas_call(
        functools.partial(_attn_prompt_kernel, tq=tq),
        grid=(bsz, n_heads // hps),
        in_specs=[head_spec, head_spec, head_spec],
        out_specs=pl.BlockSpec((1, s, hps * HEAD_DIM), lambda b, hp: (b, 0, hp)),
        out_shape=jax.ShapeDtypeStruct((bsz, s, n_heads * HEAD_DIM), F32),
        compiler_params=_params(("arbitrary", "arbitrary")),
        name="attn_prompt",
    )(qp, kp, vp)


def _attn_decode_kernel(*refs, n_pages_step, n_heads, t, group_heads):
    g = n_pages_step
    qn_ref, kn_ref, vn_ref, lfn_ref, exp_ref, rep_ref = refs[1:7]
    k_refs = refs[7:7 + g]
    v_refs = refs[7 + g:7 + 2 * g]
    lf_refs = refs[7 + 2 * g:7 + 3 * g]
    o_ref = refs[7 + 3 * g]
    qexp_ref, m_ref, l_ref, acc_ref, carry_ref = refs[8 + 3 * g:]
    ht = n_heads * t
    e_b = n_heads * HEAD_DIM
    page = k_refs[0].shape[1]
    s_idx = pl.program_id(1)

    @pl.when(s_idx == 0)
    def _():
        spread = _dot(rep_ref[...], qn_ref[0].astype(BF16).astype(F32))
        keep = (lax.broadcasted_iota(jnp.int32, (ht, e_b), 0) // t
                == lax.broadcasted_iota(jnp.int32, (ht, e_b), 1) // HEAD_DIM)
        qexp_ref[...] = jnp.where(keep, spread, 0.0).astype(BF16)
        m_ref[...] = jnp.full(m_ref.shape, NEG_BIG, F32)
        l_ref[...] = jnp.zeros(l_ref.shape, F32)
        acc_ref[...] = jnp.zeros(acc_ref.shape, F32)
        carry_ref[...] = jnp.zeros(carry_ref.shape, F32)

    row = lax.broadcasted_iota(jnp.int32, (page, page), 0)
    col = lax.broadcasted_iota(jnp.int32, (page, page), 1)
    upper = jnp.where(row <= col, 1.0, 0.0).astype(BF16)

    def rows_per_head(a):
        return jnp.broadcast_to(a[:, None, :], (n_heads, t, a.shape[-1])).reshape(ht, a.shape[-1])

    def cumsum_keys(c, lf):
        for piece in _split3(lf):
            c = c + _dot(piece, upper)
        return c

    def page_bias(lf_t):
        lf2 = lf_t * LOG2E
        before = carry_ref[...]
        carry_ref[...] = before + jnp.sum(lf2, axis=1, keepdims=True)
        return cumsum_keys(jnp.broadcast_to(before, (n_heads, page)), lf2)

    gr, gc = group_heads * t, group_heads * HEAD_DIM
    n_groups = n_heads // group_heads
    rows = lambda a, c: a[c * gr:(c + 1) * gr]
    q_group = lambda c: qexp_ref[pl.ds(c * gr, gr), pl.ds(c * gc, gc)]
    slab = lambda ref, c: ref[pl.ds(c * gc, gc), :].astype(BF16)
    by_group = lambda f: jnp.concatenate([f(c) for c in range(n_groups)], axis=0)

    def update(s, pv_t):
        m_old = m_ref[...]
        m_new = jnp.maximum(m_old, jnp.max(s, axis=1, keepdims=True))
        p = jnp.exp2(s - m_new)
        alpha = jnp.exp2(m_old - m_new)
        m_ref[...] = m_new
        l_ref[...] = alpha * l_ref[...] + jnp.sum(p, axis=1, keepdims=True)
        alpha_row = jnp.broadcast_to(alpha, (ht, ht)).T[0:1, :]
        p = p.astype(BF16)
        for c in range(n_groups):
            acc_ref[c, :, pl.ds(0, gr)] = (alpha_row[:, c * gr:(c + 1) * gr] * acc_ref[c, :, pl.ds(0, gr)]
                                           + pv_t(p, c))

    s = jnp.concatenate(
        [by_group(lambda c: _dot(q_group(c), slab(k_refs[i], c))) - rows_per_head(page_bias(lf_refs[i][...]))
         for i in range(g)], axis=1)
    update(s, lambda p, c: sum(
        _dot_nt(jnp.concatenate([slab(v_refs[i], c), slab(v_refs[i + 1], c)], axis=1),
                rows(p, c)[:, i * page:(i + 2) * page]) for i in range(0, g, 2)))

    @pl.when(s_idx == pl.num_programs(1) - 1)
    def _():
        pad = lambda a: jnp.concatenate([a, jnp.zeros((page - t, a.shape[1]), F32)], axis=0)
        lexp = jnp.zeros((page, ht), F32)
        for piece in _split3(pad(lfn_ref[0]) * LOG2E):
            lexp = lexp + _dot(piece, exp_ref[...])
        c_new = cumsum_keys(rows_per_head(jnp.broadcast_to(carry_ref[...], (n_heads, page))), lexp.T)
        k_pad = pad(kn_ref[0]).astype(BF16)
        v_pad = pad(vn_ref[0]).astype(BF16)
        s_new = by_group(lambda c: _dot_nt(q_group(c), k_pad[:, c * gc:(c + 1) * gc])) - c_new
        key = lax.broadcasted_iota(jnp.int32, (ht, page), 1)
        tok = lax.broadcasted_iota(jnp.int32, (ht, page), 0) % t
        tall = lambda a: jnp.concatenate([a, jnp.zeros((ht - a.shape[0], a.shape[1]), F32)], axis=0)
        update(jnp.where(key <= tok, s_new, NEG_BIG),
               lambda p, c: tall(_dot(rows(p, c), v_pad[:, c * gc:(c + 1) * gc])).T[:, :gr])
        keep = (lax.broadcasted_iota(jnp.int32, (gr, gc), 0) // t
                == lax.broadcasted_iota(jnp.int32, (gr, gc), 1) // HEAD_DIM)
        outs = []
        for c in range(n_groups):
            o = acc_ref[c].T[:gr, :] / rows(l_ref[...], c)
            outs.append(jnp.sum(jnp.where(keep, o, 0.0).reshape(group_heads, t, gc), axis=0))
        o_ref[0] = jnp.concatenate(outs, axis=1)


def _attn_decode(qn, k_new, v_new, lf_new, cache_k, cache_v, cache_logf, page_table, *, n_heads, pages_per_step):
    bsz, t, e_b = qn.shape
    n_pool, page, _, _ = cache_k.shape
    n_pages = page_table.shape[1]
    g = pages_per_step
    ht = n_heads * t
    gh = MXU_DEPTH // HEAD_DIM
    assert n_pages % g == 0 and g % 2 == 0 and ht == page == LANES and t % SUBLANES == 0 and n_heads % gh == 0
    ck = jnp.transpose(cache_k, (0, 2, 3, 1)).reshape(n_pool, e_b, page)
    cv = jnp.transpose(cache_v, (0, 2, 3, 1)).reshape(n_pool, e_b, page)
    clf = jnp.transpose(cache_logf, (0, 2, 1))
    expand = jnp.asarray((np.arange(n_heads)[:, None] == (np.arange(ht)[None, :] // t)).astype(np.float32), BF16)
    repeat = jnp.asarray((np.arange(ht)[:, None] % t == np.arange(t)[None, :]).astype(np.float32))
    tok = lambda width: pl.BlockSpec((1, t, width), lambda b, s, pt: (b, 0, 0))
    page_spec = lambda rows, i: pl.BlockSpec((None, rows, page), lambda b, s, pt, i=i: (pt[b, s * g + i], 0, 0))
    in_specs = ([tok(e_b), tok(e_b), tok(e_b), tok(n_heads),
                 pl.BlockSpec(expand.shape, lambda b, s, pt: (0, 0)),
                 pl.BlockSpec(repeat.shape, lambda b, s, pt: (0, 0))]
                + [page_spec(e_b, i) for i in range(g)] * 2
                + [page_spec(n_heads, i) for i in range(g)])
    return pl.pallas_call(
        functools.partial(_attn_decode_kernel, n_pages_step=g, n_heads=n_heads, t=t, group_heads=gh),
        grid_spec=pltpu.PrefetchScalarGridSpec(
            num_scalar_prefetch=1,
            grid=(bsz, n_pages // g),
            in_specs=in_specs,
            out_specs=tok(e_b),
            scratch_shapes=[pltpu.VMEM((ht, e_b), BF16), pltpu.VMEM((ht, 1), F32), pltpu.VMEM((ht, 1), F32),
                            pltpu.VMEM((n_heads // gh, gh * HEAD_DIM, LANES), F32),
                            pltpu.VMEM((n_heads, 1), F32)]),
        out_shape=jax.ShapeDtypeStruct((bsz, t, e_b), F32),
        compiler_params=_params(("arbitrary", "arbitrary")),
        name="attn_decode",
    )(page_table, qn, k_new, v_new, lf_new, expand, repeat, *([ck] * g), *([cv] * g), *([clf] * g))


def _fox_out_kernel(o_ref, zs_ref, x_ref, p_ref, wout_ref, gw_ref, pw_ref, y_ref):
    m = x_ref.shape[0] * x_ref.shape[1]
    two_d = lambda r: r[...].reshape(m, r.shape[-1])
    gated = (two_d(o_ref) * two_d(zs_ref)).astype(BF16)
    x1 = two_d(x_ref) + _wdot(gated, wout_ref)
    y_ref[...] = _ple_tail(x1, two_d(p_ref), gw_ref, pw_ref).reshape(y_ref.shape)


def _fox_out(o, zs, x, p_all, layer, w_out, gate_w, ple_w, *, tm):
    bsz, s, d = x.shape
    pd = p_all.shape[-1]
    if s % tm == 0:
        nb, ns = 1, tm
    else:
        nb, ns = bsz, s
    tok = lambda width: pl.BlockSpec((nb, ns, width), lambda b, i: (b, i, 0))
    weights = (w_out, gate_w, ple_w)
    return pl.pallas_call(
        _fox_out_kernel,
        grid=(bsz // nb, s // ns),
        in_specs=[tok(o.shape[-1]), tok(zs.shape[-1]), tok(d),
                  pl.BlockSpec((None, nb, ns, pd), lambda b, i: (layer, b, i, 0))]
                 + [_const_spec(w.shape) for w in weights],
        out_specs=tok(d),
        out_shape=jax.ShapeDtypeStruct((bsz, s, d), F32),
        compiler_params=_params(("arbitrary", "arbitrary")),
        name="fox_out",
    )(o, zs, x, p_all, *weights)


TM_PROMPT = 256
TM_KV = 512
KV_PARTS = 4
TQ_PROMPT = 512
ATTN_HEADS_PER_STEP = 2
PAGES_PER_STEP = 16


def kernel(x_prompt, x_sample, state_conv, cache_k, cache_v, cache_logf, page_table, p_prompt, p_sample,
           a_norm, a_w_in, a_conv_w, a_conv_b, a_ln_g, a_ln_b, a_w_out, kv_norm, kv_w, kv_k_norm, kv_f_bias,
           b_norm, b_w_q, b_q_norm, b_w_out, ple_w, ple_gate_w):
    n_a = a_norm.shape[0]
    n_b = b_norm.shape[0]
    n_heads = kv_f_bias.shape[0]
    e_b = n_heads * HEAD_DIM
    d = x_prompt.shape[-1]
    row = lambda v: v.reshape(1, -1)
    bf = _lane_tiles

    a_w_in16, a_w_out16, b_w_q16, b_w_out16 = bf(a_w_in), bf(a_w_out), bf(b_w_q), bf(b_w_out)
    ple_w16, gate_w16 = bf(ple_w), bf(ple_gate_w)
    w_kv16 = bf(kv_w[:, :2 * e_b])
    w_f16 = bf(jnp.pad(kv_w[:, 2 * e_b:], ((0, 0), (0, LANES - n_heads))))
    f_bias = jnp.pad(row(kv_f_bias), ((0, 0), (0, LANES - n_heads)))
    k_gain = jnp.tile(row(kv_k_norm), (1, n_heads))
    assert e_b % MXU_DEPTH == 0 and MXU_DEPTH % HEAD_DIM == 0
    bd = _block_diag_ones()
    tm = min(TM_PROMPT, x_prompt.shape[1])
    tq = min(TQ_PROMPT, x_prompt.shape[1])

    def trunk(x, p_all, prev, prompt):
        conv_states = []
        if prev is not None:
            prev = jnp.transpose(prev, (0, 2, 1, 3))
        for i in range(n_a):
            x, st = _conv_layer(x, p_all, i, prev, row(a_norm[i]), a_w_in16[i],
                                a_conv_w[i], row(a_conv_b[i]), row(a_ln_g[i]), row(a_ln_b[i]), a_w_out16[i],
                                gate_w16[i], ple_w16[i], tm=tm)
            conv_states.append(st)
        kv = _kv_proj(x, row(kv_norm), w_kv16, w_f16, f_bias, k_gain, bd, n_heads=n_heads, prompt=prompt,
                      tm=min(TM_KV, x.shape[1]))
        k_new, v_new, lf_new = kv[:3]
        for j in range(n_b):
            q_gain = jnp.tile(row(b_q_norm[j]), (1, n_heads)) * (LOG2E * HEAD_DIM ** -0.5)
            if prompt:
                qp, zs = _fox_q(x, kv[3], row(b_norm[j]), b_w_q16[j], q_gain, bd, n_heads=n_heads, prompt=True,
                                tm=min(TM_KV, x.shape[1]))
                o = _attn_prompt(qp, kv[4], kv[5], tq=tq, heads_per_step=ATTN_HEADS_PER_STEP)
            else:
                qn, zs = _fox_q(x, None, row(b_norm[j]), b_w_q16[j], q_gain, bd, n_heads=n_heads, prompt=False, tm=tm)
                o = _attn_decode(qn, k_new, v_new, lf_new, cache_k, cache_v, cache_logf, page_table,
                                 n_heads=n_heads, pages_per_step=min(PAGES_PER_STEP, page_table.shape[1]))
            x = _fox_out(o, zs, x, p_all, n_a + j, b_w_out16[j], gate_w16[n_a + j], ple_w16[n_a + j], tm=tm)
        shape4 = k_new.shape[:2] + (n_heads, HEAD_DIM)
        conv_state = jnp.transpose(jnp.stack(conv_states, axis=0), (0, 2, 1, 3))
        return x, conv_state, k_new.reshape(shape4), v_new.reshape(shape4), lf_new

    y_p, conv_p, k_p, v_p, lf_p = trunk(x_prompt, p_prompt, None, True)
    y_s, conv_s, k_s, v_s, lf_s = trunk(x_sample, p_sample, state_conv, False)
    return (y_p, y_s, conv_p, k_p, v_p, lf_p, conv_s, k_s, v_s, lf_s)
```

```python
import functools
import math

import numpy as np
import jax
import jax.numpy as jnp
from jax import lax
from jax.experimental import pallas as pl
from jax.experimental.pallas import tpu as pltpu

F32 = jnp.float32
BF16 = jnp.bfloat16

EPS = 1e-6
CONV_W = 31
HALO = CONV_W - 1
HEAD_DIM = 64
LANES = 128
SUBLANES = 8
MXU_DEPTH = 256
HALO_PAD = 32
LANE_TILE_COLS = 1024
CONV_TIME_CHUNK = 8
CONV_LANE_CHUNK = 2 * LANES
LOG2E = math.log2(math.e)
NEG_BIG = -1e30
VMEM_LIMIT = 56 * 1024 * 1024


def _dot(a, b):
    return jnp.dot(a, b, preferred_element_type=F32)


def _dot_nt(a, b):
    return lax.dot_general(a, b, (((1,), (1,)), ((), ())), preferred_element_type=F32)


def _lane_tiles_kernel(w_ref, o_ref):
    for j in range(o_ref.shape[1]):
        o_ref[0, j] = w_ref[0, :, pl.ds(j * LANES, LANES)].astype(BF16)


def _lane_tiles(w):
    *lead, k, n = w.shape
    n_lead = math.prod(lead)
    cols = min(n, LANE_TILE_COLS)
    assert n % cols == 0 and cols % LANES == 0
    tiles = pl.pallas_call(
        _lane_tiles_kernel,
        grid=(n_lead, n // cols),
        in_specs=[pl.BlockSpec((1, k, cols), lambda i, c: (i, 0, c))],
        out_specs=pl.BlockSpec((1, cols // LANES, k, LANES), lambda i, c: (i, c, 0, 0)),
        out_shape=jax.ShapeDtypeStruct((n_lead, n // LANES, k, LANES), BF16),
        compiler_params=_params(("arbitrary", "arbitrary")),
        name="lane_tiles",
    )(w.reshape(n_lead, k, n))
    return tiles.reshape(*lead, n // LANES, k, LANES)


def _wdot(a, w_ref, col0=0, ncols=None, lead=()):
    first = col0 // LANES
    count = w_ref.shape[-3] - first if ncols is None else ncols // LANES
    tiles = [w_ref[lead + (j,)] for j in range(first, first + count)]
    return _dot(a, tiles[0] if count == 1 else jnp.concatenate(tiles, axis=1))


def _split3(x):
    p0 = x.astype(BF16)
    r = x - p0.astype(F32)
    p1 = r.astype(BF16)
    r = r - p1.astype(F32)
    return p0, p1, r.astype(BF16)


def _split2(x):
    p0 = x.astype(BF16)
    return p0, (x - p0.astype(F32)).astype(BF16)


def _rms(x, g):
    return x * lax.rsqrt(jnp.mean(x * x, axis=-1, keepdims=True) + EPS) * g


def _sigmoid(x):
    return 1.0 / (1.0 + jnp.exp(-x))


def _silu(x):
    return x * _sigmoid(x)


def _log_sigmoid(x):
    return jnp.minimum(x, 0.0) - jnp.log1p(jnp.exp(-jnp.abs(x)))


def _const_spec(shape):
    nd = len(shape)
    return pl.BlockSpec(shape, lambda *_: (0,) * nd, pipeline_mode=pl.Buffered(1))


def _params(sem):
    return pltpu.CompilerParams(dimension_semantics=sem, vmem_limit_bytes=VMEM_LIMIT)


def _ple_tail(x1, p, gw_ref, pw_ref):
    gate = _sigmoid(_wdot(x1.astype(BF16), gw_ref))
    return x1 + gate * _wdot(p.astype(BF16), pw_ref)


def _conv_layer_kernel(*refs, nb, t, time_chunk, lane_chunk, has_prev):
    if has_prev:
        (x_ref, p_ref, prev_ref, g_ref, win_ref, cw_ref, cb_ref, lng_ref, lnb_ref, wout_ref,
         gw_ref, pw_ref, y_ref, st_ref, xs_ref, ps_ref, ys_ref, uext_ref, c_ref, zs_ref) = refs
    else:
        (x_ref, p_ref, g_ref, win_ref, cw_ref, cb_ref, lng_ref, lnb_ref, wout_ref,
         gw_ref, pw_ref, y_ref, st_ref, xs_ref, ps_ref, ys_ref, uext_ref, c_ref, zs_ref) = refs
    e = cw_ref.shape[-1]

    for tt in range(t):
        xs_ref[pl.ds(tt * nb, nb), :] = x_ref[:, tt, :]
        ps_ref[pl.ds(tt * nb, nb), :] = p_ref[:, tt, :]

    if has_prev:
        uext_ref[pl.ds(HALO_PAD - HALO, HALO)] = prev_ref[...]
    else:
        s = pl.program_id(1)

        @pl.when(s == 0)
        def _():
            uext_ref[pl.ds(0, HALO_PAD)] = jnp.zeros((HALO_PAD, nb, e), F32)

        @pl.when(s > 0)
        def _():
            uext_ref[pl.ds(0, HALO_PAD)] = uext_ref[pl.ds(t, HALO_PAD)]

    h = _rms(xs_ref[...], g_ref[...]).astype(BF16)
    for l0 in range(0, e, lane_chunk):
        lanes = pl.ds(l0, lane_chunk)
        a = _wdot(h, win_ref, l0, lane_chunk)
        b = _wdot(h, win_ref, e + l0, lane_chunk)
        uext_ref[pl.ds(HALO_PAD, t), :, lanes] = (a * _sigmoid(b)).reshape(t, nb, lane_chunk)
        for t0 in range(0, t, time_chunk):
            for b0 in range(0, nb, SUBLANES):
                acc = jnp.broadcast_to(cb_ref[:, lanes], (time_chunk, SUBLANES, lane_chunk))
                for k in range(CONV_W):
                    win = uext_ref[pl.ds(HALO_PAD - HALO + t0 + k, time_chunk), pl.ds(b0, SUBLANES), lanes]
                    acc = acc + win * cw_ref[k, :, lanes]
                c_ref[pl.ds(t0, time_chunk), pl.ds(b0, SUBLANES), lanes] = acc
        zs_ref[:, lanes] = _silu(_wdot(h, win_ref, 2 * e + l0, lane_chunk))

    if has_prev:
        st_ref[...] = uext_ref[pl.ds(HALO_PAD + t - HALO, HALO)]
    else:
        @pl.when(pl.program_id(1) == pl.num_programs(1) - 1)
        def _():
            st_ref[...] = uext_ref[pl.ds(HALO_PAD + t - HALO, HALO)]

    th = t // 2
    for half in range(2):
        rows = pl.ds(half * th * nb, th * nb)
        c = c_ref[pl.ds(half * th, th)].reshape(th * nb, e)
        mu = jnp.mean(c, axis=-1, keepdims=True)
        cc = c - mu
        ln = cc * lax.rsqrt(jnp.mean(cc * cc, axis=-1, keepdims=True) + EPS) * lng_ref[...] + lnb_ref[...]
        gated = (_silu(ln) * zs_ref[rows, :]).astype(BF16)
        x1 = xs_ref[rows, :] + _wdot(gated, wout_ref)
        ys_ref[rows, :] = _ple_tail(x1, ps_ref[rows, :], gw_ref, pw_ref)
    for tt in range(t):
        y_ref[:, tt, :] = ys_ref[pl.ds(tt * nb, nb), :]


def _conv_layer(x, p_all, layer, prev, g, w_in, conv_w, conv_b, ln_g, ln_b, w_out, gate_w, ple_w, *, tm):
    bsz, s, d = x.shape
    e = conv_w.shape[1]
    pd = p_all.shape[-1]
    has_prev = prev is not None
    conv_w8 = jnp.broadcast_to(conv_w[:, None, :], (CONV_W, SUBLANES, e))
    conv_b8 = jnp.broadcast_to(conv_b, (SUBLANES, e))
    weights = (g, w_in, conv_w8, conv_b8, ln_g, ln_b, w_out, gate_w, ple_w)
    if has_prev:
        nb, t = bsz, s
    else:
        nb, t = SUBLANES, tm // SUBLANES
        assert s % t == 0
    time_chunk = min(t, CONV_TIME_CHUNK)
    assert bsz % nb == 0 and nb % SUBLANES == 0 and t % time_chunk == 0 and t % 2 == 0
    x_spec = pl.BlockSpec((nb, t, d), lambda gb, i: (gb, i, 0))
    st_spec = pl.BlockSpec((HALO, nb, e), lambda gb, i: (0, gb, 0))
    in_specs = [x_spec, pl.BlockSpec((None, nb, t, pd), lambda gb, i: (layer, gb, i, 0))]
    args = (x, p_all)
    if has_prev:
        in_specs.append(pl.BlockSpec((None, HALO, nb, e), lambda gb, i: (layer, 0, gb, 0)))
        args += (prev,)
    kern = functools.partial(_conv_layer_kernel, nb=nb, t=t, time_chunk=time_chunk,
                             lane_chunk=CONV_LANE_CHUNK, has_prev=has_prev)
    m = nb * t
    return pl.pallas_call(
        kern,
        grid=(bsz // nb, s // t),
        in_specs=in_specs + [_const_spec(w.shape) for w in weights],
        out_specs=[x_spec, st_spec],
        out_shape=[jax.ShapeDtypeStruct((bsz, s, d), F32), jax.ShapeDtypeStruct((HALO, bsz, e), F32)],
        scratch_shapes=[pltpu.VMEM((m, d), F32), pltpu.VMEM((m, pd), F32), pltpu.VMEM((m, d), F32),
                        pltpu.VMEM((HALO_PAD + t, nb, e), F32), pltpu.VMEM((t, nb, e), F32),
                        pltpu.VMEM((m, e), F32)],
        compiler_params=_params(("arbitrary", "arbitrary")),
        name="conv_layer_decode" if has_prev else "conv_layer_prompt",
    )(*args, *weights)


def _head_norm(raw, bd_ref, gain):
    sq = (raw * raw).astype(BF16)
    ms = jnp.concatenate([_wdot(sq[:, c0:c0 + MXU_DEPTH], bd_ref) for c0 in range(0, raw.shape[1], MXU_DEPTH)],
                         axis=1) * (1.0 / HEAD_DIM)
    return raw * lax.rsqrt(ms + EPS) * gain


def _head_tiles(arr, n_heads, lane, extra):
    for hd in range(n_heads):
        pair = arr[:, (hd // 2) * LANES:(hd // 2 + 1) * LANES]
        if hd % 2 == 1:
            pair = pltpu.roll(pair, HEAD_DIM, axis=1)
        yield hd, jnp.where(lane < HEAD_DIM, pair, extra(hd))


def _bias_lanes(c, lane, piece_lane, ones_lane):
    pieces = [p.astype(F32) for p in _split3(c)]
    base = jnp.where((lane >= ones_lane) & (lane < ones_lane + 3), 1.0, 0.0)

    def extra(hd):
        tile = base
        for i, piece in enumerate(pieces):
            tile = jnp.where(lane == piece_lane + i, pltpu.roll(piece, (piece_lane + i - hd) % LANES, axis=1), tile)
        return tile

    return extra


def _kv_kernel(*refs, n_heads, prompt, n_parts):
    if prompt:
        (x_ref, g_ref, wkv_ref, wf_ref, fb_ref, kg_ref, bd_ref,
         k_ref, v_ref, lf_ref, c_ref, kp_ref, vp_ref, carry_ref) = refs

        @pl.when(pl.program_id(1) == 0)
        def _():
            carry_ref[...] = jnp.zeros(carry_ref.shape, F32)
    else:
        (x_ref, g_ref, wkv_ref, wf_ref, fb_ref, kg_ref, bd_ref, k_ref, v_ref, lf_ref) = refs
    e_b = n_heads * HEAD_DIM
    m = x_ref.shape[-2] // n_parts
    row = lax.broadcasted_iota(jnp.int32, (m, m), 0)
    col = lax.broadcasted_iota(jnp.int32, (m, m), 1)
    tri = jnp.where(col <= row, 1.0, 0.0).astype(BF16)
    lane = lax.broadcasted_iota(jnp.int32, (m, LANES), 1)
    ones_col = jnp.where(lane == HEAD_DIM, 1.0, 0.0)
    for part in range(n_parts):
        rows = pl.ds(part * m, m)
        h = _rms(x_ref[0, rows, :], g_ref[...]).astype(BF16)
        kn = _head_norm(_wdot(h, wkv_ref, 0, e_b), bd_ref, kg_ref[...])
        v = _wdot(h, wkv_ref, e_b, e_b)
        logf = _log_sigmoid(_wdot(h, wf_ref) + fb_ref[...])
        k_ref[0, rows, :] = kn
        v_ref[0, rows, :] = v
        lf_ref[0, rows, :] = logf[:, :n_heads]
        if not prompt:
            continue
        c = carry_ref[...]
        for piece in _split3(logf * LOG2E):
            c = c + _dot(tri, piece)
        carry_ref[...] = c[m - 1:m, :]
        c_ref[0, rows, :] = c
        for hd, tile in _head_tiles(kn, n_heads, lane, _bias_lanes(-c, lane, HEAD_DIM, HEAD_DIM + 3)):
            kp_ref[0, hd, rows, :] = tile.astype(BF16)
        for hd, tile in _head_tiles(v, n_heads, lane, lambda hd: ones_col):
            vp_ref[0, hd, rows, :] = tile.astype(BF16)


def _block_diag_ones():
    idx = np.arange(MXU_DEPTH) // HEAD_DIM
    ones = (idx[:, None] == idx[None, :]).astype(np.float32)
    return jnp.asarray(np.moveaxis(ones.reshape(MXU_DEPTH, MXU_DEPTH // LANES, LANES), 1, 0), BF16)


def _kv_proj(x, g, w_kv, w_f, f_bias, k_gain, bd, *, n_heads, prompt, tm):
    bsz, s, d = x.shape
    e_b = n_heads * HEAD_DIM
    outs = [jax.ShapeDtypeStruct((bsz, s, e_b), F32), jax.ShapeDtypeStruct((bsz, s, e_b), F32),
            jax.ShapeDtypeStruct((bsz, s, n_heads), F32)]
    weights = [g, w_kv, w_f, f_bias, k_gain, bd]
    if prompt:
        assert s % tm == 0
        grid = (bsz, s // tm)
        tok = lambda width: pl.BlockSpec((1, tm, width), lambda b, i: (b, i, 0))
        head_spec = pl.BlockSpec((1, n_heads, tm, LANES), lambda b, i: (b, 0, i, 0))
        out_specs = [tok(e_b), tok(e_b), tok(n_heads), tok(LANES), head_spec, head_spec]
        outs += [jax.ShapeDtypeStruct((bsz, s, LANES), F32),
                 jax.ShapeDtypeStruct((bsz, n_heads, s, LANES), BF16),
                 jax.ShapeDtypeStruct((bsz, n_heads, s, LANES), BF16)]
        x_spec = tok(d)
        scratch = [pltpu.VMEM((1, LANES), F32)]
        sem = ("arbitrary", "arbitrary")
    else:
        grid = (1,)
        x2 = x.reshape(1, bsz * s, d)
        full = lambda width: pl.BlockSpec((1, bsz * s, width), lambda i: (0, 0, 0))
        outs = [jax.ShapeDtypeStruct((1, bsz * s, o.shape[-1]), F32) for o in outs]
        out_specs = [full(e_b), full(e_b), full(n_heads)]
        x_spec = full(d)
        x = x2
        scratch = []
        sem = ("arbitrary",)
    res = pl.pallas_call(
        functools.partial(_kv_kernel, n_heads=n_heads, prompt=prompt, n_parts=KV_PARTS if prompt else 1),
        grid=grid,
        in_specs=[x_spec] + [_const_spec(w.shape) for w in weights],
        out_specs=out_specs,
        out_shape=outs,
        scratch_shapes=scratch,
        compiler_params=_params(sem),
        name="kv_proj_prompt" if prompt else "kv_proj_decode",
    )(x, *weights)
    if not prompt:
        res = [r.reshape(bsz, s, r.shape[-1]) for r in res]
    return res


def _fox_q_kernel(*refs, n_heads, prompt, n_parts):
    if prompt:
        x_ref, c_ref, g_ref, wq_ref, qg_ref, bd_ref, qp_ref, zs_ref = refs
    else:
        x_ref, g_ref, wq_ref, qg_ref, bd_ref, qn_ref, zs_ref = refs
    e_b = n_heads * HEAD_DIM
    m = x_ref.shape[-2] // n_parts
    lane = lax.broadcasted_iota(jnp.int32, (m, LANES), 1)
    for part in range(n_parts):
        rows = pl.ds(part * m, m)
        h = _rms(x_ref[0, rows, :], g_ref[...]).astype(BF16)
        qn = _head_norm(_wdot(h, wq_ref, 0, e_b), bd_ref, qg_ref[...])
        zs_ref[0, rows, :] = _silu(_wdot(h, wq_ref, e_b, e_b)).astype(BF16)
        if not prompt:
            qn_ref[0, rows, :] = qn
            continue
        extra = _bias_lanes(c_ref[0, rows, :], lane, HEAD_DIM + 3, HEAD_DIM)
        for hd, tile in _head_tiles(qn, n_heads, lane, extra):
            qp_ref[0, hd, rows, :] = tile.astype(BF16)


def _fox_q(x, c, g, w_q, q_gain, bd, *, n_heads, prompt, tm):
    bsz, s, d = x.shape
    e_b = n_heads * HEAD_DIM
    if prompt:
        grid = (bsz, s // tm)
        tok = lambda width: pl.BlockSpec((1, tm, width), lambda b, i: (b, i, 0))
        weights = [g, w_q, q_gain, bd]
        args = (x, c)
        in_specs = [tok(d), tok(LANES)]
        out_specs = [pl.BlockSpec((1, n_heads, tm, LANES), lambda b, i: (b, 0, i, 0)), tok(e_b)]
        outs = [jax.ShapeDtypeStruct((bsz, n_heads, s, LANES), BF16), jax.ShapeDtypeStruct((bsz, s, e_b), BF16)]
        sem = ("arbitrary", "arbitrary")
    else:
        grid = (1,)
        full = lambda width: pl.BlockSpec((1, bsz * s, width), lambda i: (0, 0, 0))
        weights = [g, w_q, q_gain, bd]
        args = (x.reshape(1, bsz * s, d),)
        in_specs = [full(d)]
        out_specs = [full(e_b), full(e_b)]
        outs = [jax.ShapeDtypeStruct((1, bsz * s, e_b), F32), jax.ShapeDtypeStruct((1, bsz * s, e_b), BF16)]
        sem = ("arbitrary",)
    res = pl.pallas_call(
        functools.partial(_fox_q_kernel, n_heads=n_heads, prompt=prompt, n_parts=KV_PARTS if prompt else 1),
        grid=grid,
        in_specs=in_specs + [_const_spec(w.shape) for w in weights],
        out_specs=out_specs,
        out_shape=outs,
        compiler_params=_params(sem),
        name="fox_q_prompt" if prompt else "fox_q_decode",
    )(*args, *weights)
    if not prompt:
        res = [r.reshape(bsz, s, e_b) for r in res]
    return res


def _attn_prompt_kernel(q_ref, k_ref, v_ref, o_ref, *, tq):
    n_h, s_len = q_ref.shape[1], q_ref.shape[2]
    th = tq // 2
    lane = lax.broadcasted_iota(jnp.int32, (tq, LANES), 1)
    tri_full = (lax.broadcasted_iota(jnp.int32, (tq, th), 1) <= lax.broadcasted_iota(jnp.int32, (tq, th), 0))
    tri_half = tri_full[:th]

    def fold(q, hh, k0, nk, m_old, acc, mask):
        s = _dot_nt(q, k_ref[0, hh, pl.ds(k0, nk), :])
        if mask is not None:
            s = jnp.where(mask, s, NEG_BIG)
        m_new = jnp.maximum(m_old, jnp.max(s, axis=-1, keepdims=True))
        p = jnp.exp2(s - m_new).astype(BF16)
        pv = _dot(p, v_ref[0, hh, pl.ds(k0, nk), :])
        return m_new, jnp.exp2(m_old - m_new) * acc + pv

    for qi in range(s_len // tq):
        q0 = qi * tq
        outs = []
        state = [(jnp.full((tq, 1), NEG_BIG, F32), jnp.zeros((tq, LANES), F32)) for _ in range(n_h)]
        for j in range(qi):
            for hh in range(n_h):
                state[hh] = fold(q_ref[0, hh, pl.ds(q0, tq), :], hh, j * tq, tq, *state[hh], None)
        for hh in range(n_h):
            m, acc = fold(q_ref[0, hh, pl.ds(q0, tq), :], hh, q0, th, *state[hh], tri_full)
            m_lo, acc_lo = fold(q_ref[0, hh, pl.ds(q0 + th, th), :], hh, q0 + th, th, m[th:], acc[th:], tri_half)
            acc = jnp.concatenate([acc[:th], acc_lo], axis=0)
            denom = jnp.sum(jnp.where(lane == HEAD_DIM, acc, 0.0), axis=-1, keepdims=True)
            outs.append(acc / denom)
        for pair in range(n_h // 2):
            o_ref[0, pl.ds(q0, tq), pl.ds(pair * LANES, LANES)] = jnp.where(
                lane < HEAD_DIM, outs[2 * pair], pltpu.roll(outs[2 * pair + 1], HEAD_DIM, axis=1))


def _attn_prompt(qp, kp, vp, *, tq, heads_per_step):
    bsz, n_heads, s, _ = qp.shape
    hps = heads_per_step
    assert s % tq == 0 and n_heads % hps == 0 and hps % 2 == 0 and tq % (2 * SUBLANES) == 0
    head_spec = pl.BlockSpec((1, hps, s, LANES), lambda b, hp: (b, hp, 0, 0))
    return pl.pallas_call(
        functools.partial(_attn_prompt_kernel, tq=tq),
        grid=(bsz, n_heads // hps),
        in_specs=[head_spec, head_spec, head_spec],
        out_specs=pl.BlockSpec((1, s, hps * HEAD_DIM), lambda b, hp: (b, 0, hp)),
        out_shape=jax.ShapeDtypeStruct((bsz, s, n_heads * HEAD_DIM), F32),
        compiler_params=_params(("arbitrary", "arbitrary")),
        name="attn_prompt",
    )(qp, kp, vp)


def _attn_decode_kernel(*refs, n_pages_step, n_heads, t, group_heads):
    g = n_pages_step
    qn_ref, kn_ref, vn_ref, lfn_ref, exp_ref, rep_ref = refs[1:7]
    k_refs = refs[7:7 + g]
    v_refs = refs[7 + g:7 + 2 * g]
    lf_refs = refs[7 + 2 * g:7 + 3 * g]
    o_ref = refs[7 + 3 * g]
    qexp_ref, m_ref, l_ref, acc_ref, carry_ref = refs[8 + 3 * g:]
    ht = n_heads * t
    e_b = n_heads * HEAD_DIM
    page = k_refs[0].shape[1]
    s_idx = pl.program_id(1)

    @pl.when(s_idx == 0)
    def _():
        spread = _dot(rep_ref[...], qn_ref[0].astype(BF16).astype(F32))
        keep = (lax.broadcasted_iota(jnp.int32, (ht, e_b), 0) // t
                == lax.broadcasted_iota(jnp.int32, (ht, e_b), 1) // HEAD_DIM)
        qexp_ref[...] = jnp.where(keep, spread, 0.0).astype(BF16)
        m_ref[...] = jnp.full(m_ref.shape, NEG_BIG, F32)
        l_ref[...] = jnp.zeros(l_ref.shape, F32)
        acc_ref[...] = jnp.zeros(acc_ref.shape, F32)
        carry_ref[...] = jnp.zeros(carry_ref.shape, F32)

    row = lax.broadcasted_iota(jnp.int32, (page, page), 0)
    col = lax.broadcasted_iota(jnp.int32, (page, page), 1)
    upper = jnp.where(row <= col, 1.0, 0.0).astype(BF16)

    def rows_per_head(a):
        return jnp.broadcast_to(a[:, None, :], (n_heads, t, a.shape[-1])).reshape(ht, a.shape[-1])

    def cumsum_keys(c, lf):
        for piece in _split3(lf):
            c = c + _dot(piece, upper)
        return c

    def page_bias(lf_t):
        lf2 = lf_t * LOG2E
        before = carry_ref[...]
        carry_ref[...] = before + jnp.sum(lf2, axis=1, keepdims=True)
        return cumsum_keys(jnp.broadcast_to(before, (n_heads, page)), lf2)

    gr, gc = group_heads * t, group_heads * HEAD_DIM
    n_groups = n_heads // group_heads
    rows = lambda a, c: a[c * gr:(c + 1) * gr]
    q_group = lambda c: qexp_ref[pl.ds(c * gr, gr), pl.ds(c * gc, gc)]
    slab = lambda ref, c: ref[pl.ds(c * gc, gc), :].astype(BF16)
    by_group = lambda f: jnp.concatenate([f(c) for c in range(n_groups)], axis=0)

    def update(s, pv_t):
        m_old = m_ref[...]
        m_new = jnp.maximum(m_old, jnp.max(s, axis=1, keepdims=True))
        p = jnp.exp2(s - m_new)
        alpha = jnp.exp2(m_old - m_new)
        m_ref[...] = m_new
        l_ref[...] = alpha * l_ref[...] + jnp.sum(p, axis=1, keepdims=True)
        alpha_row = jnp.broadcast_to(alpha, (ht, ht)).T[0:1, :]
        p = p.astype(BF16)
        for c in range(n_groups):
            acc_ref[c, :, pl.ds(0, gr)] = (alpha_row[:, c * gr:(c + 1) * gr] * acc_ref[c, :, pl.ds(0, gr)]
                                           + pv_t(p, c))

    s = jnp.concatenate(
        [by_group(lambda c: _dot(q_group(c), slab(k_refs[i], c))) - rows_per_head(page_bias(lf_refs[i][...]))
         for i in range(g)], axis=1)
    update(s, lambda p, c: sum(
        _dot_nt(jnp.concatenate([slab(v_refs[i], c), slab(v_refs[i + 1], c)], axis=1),
                rows(p, c)[:, i * page:(i + 2) * page]) for i in range(0, g, 2)))

    @pl.when(s_idx == pl.num_programs(1) - 1)
    def _():
        pad = lambda a: jnp.concatenate([a, jnp.zeros((page - t, a.shape[1]), F32)], axis=0)
        lexp = jnp.zeros((page, ht), F32)
        for piece in _split3(pad(lfn_ref[0]) * LOG2E):
            lexp = lexp + _dot(piece, exp_ref[...])
        c_new = cumsum_keys(rows_per_head(jnp.broadcast_to(carry_ref[...], (n_heads, page))), lexp.T)
        k_pad = pad(kn_ref[0]).astype(BF16)
        v_pad = pad(vn_ref[0]).astype(BF16)
        s_new = by_group(lambda c: _dot_nt(q_group(c), k_pad[:, c * gc:(c + 1) * gc])) - c_new
        key = lax.broadcasted_iota(jnp.int32, (ht, page), 1)
        tok = lax.broadcasted_iota(jnp.int32, (ht, page), 0) % t
        tall = lambda a: jnp.concatenate([a, jnp.zeros((ht - a.shape[0], a.shape[1]), F32)], axis=0)
        update(jnp.where(key <= tok, s_new, NEG_BIG),
               lambda p, c: tall(_dot(rows(p, c), v_pad[:, c * gc:(c + 1) * gc])).T[:, :gr])
        keep = (lax.broadcasted_iota(jnp.int32, (gr, gc), 0) // t
                == lax.broadcasted_iota(jnp.int32, (gr, gc), 1) // HEAD_DIM)
        outs = []
        for c in range(n_groups):
            o = acc_ref[c].T[:gr, :] / rows(l_ref[...], c)
            outs.append(jnp.sum(jnp.where(keep, o, 0.0).reshape(group_heads, t, gc), axis=0))
        o_ref[0] = jnp.concatenate(outs, axis=1)


def _attn_decode(qn, k_new, v_new, lf_new, cache_k, cache_v, cache_logf, page_table, *, n_heads, pages_per_step):
    bsz, t, e_b = qn.shape
    n_pool, page, _, _ = cache_k.shape
    n_pages = page_table.shape[1]
    g = pages_per_step
    ht = n_heads * t
    gh = MXU_DEPTH // HEAD_DIM
    assert n_pages % g == 0 and g % 2 == 0 and ht == page == LANES and t % SUBLANES == 0 and n_heads % gh == 0
    ck = jnp.transpose(cache_k, (0, 2, 3, 1)).reshape(n_pool, e_b, page)
    cv = jnp.transpose(cache_v, (0, 2, 3, 1)).reshape(n_pool, e_b, page)
    clf = jnp.transpose(cache_logf, (0, 2, 1))
    expand = jnp.asarray((np.arange(n_heads)[:, None] == (np.arange(ht)[None, :] // t)).astype(np.float32), BF16)
    repeat = jnp.asarray((np.arange(ht)[:, None] % t == np.arange(t)[None, :]).astype(np.float32))
    tok = lambda width: pl.BlockSpec((1, t, width), lambda b, s, pt: (b, 0, 0))
    page_spec = lambda rows, i: pl.BlockSpec((None, rows, page), lambda b, s, pt, i=i: (pt[b, s * g + i], 0, 0))
    in_specs = ([tok(e_b), tok(e_b), tok(e_b), tok(n_heads),
                 pl.BlockSpec(expand.shape, lambda b, s, pt: (0, 0)),
                 pl.BlockSpec(repeat.shape, lambda b, s, pt: (0, 0))]
                + [page_spec(e_b, i) for i in range(g)] * 2
                + [page_spec(n_heads, i) for i in range(g)])
    return pl.pallas_call(
        functools.partial(_attn_decode_kernel, n_pages_step=g, n_heads=n_heads, t=t, group_heads=gh),
        grid_spec=pltpu.PrefetchScalarGridSpec(
            num_scalar_prefetch=1,
            grid=(bsz, n_pages // g),
            in_specs=in_specs,
            out_specs=tok(e_b),
            scratch_shapes=[pltpu.VMEM((ht, e_b), BF16), pltpu.VMEM((ht, 1), F32), pltpu.VMEM((ht, 1), F32),
                            pltpu.VMEM((n_heads // gh, gh * HEAD_DIM, LANES), F32),
                            pltpu.VMEM((n_heads, 1), F32)]),
        out_shape=jax.ShapeDtypeStruct((bsz, t, e_b), F32),
        compiler_params=_params(("arbitrary", "arbitrary")),
        name="attn_decode",
    )(page_table, qn, k_new, v_new, lf_new, expand, repeat, *([ck] * g), *([cv] * g), *([clf] * g))


def _fox_out_kernel(o_ref, zs_ref, x_ref, p_ref, wout_ref, gw_ref, pw_ref, y_ref):
    m = x_ref.shape[0] * x_ref.shape[1]
    two_d = lambda r: r[...].reshape(m, r.shape[-1])
    gated = (two_d(o_ref) * two_d(zs_ref)).astype(BF16)
    x1 = two_d(x_ref) + _wdot(gated, wout_ref)
    y_ref[...] = _ple_tail(x1, two_d(p_ref), gw_ref, pw_ref).reshape(y_ref.shape)


def _fox_out(o, zs, x, p_all, layer, w_out, gate_w, ple_w, *, tm):
    bsz, s, d = x.shape
    pd = p_all.shape[-1]
    if s % tm == 0:
        nb, ns = 1, tm
    else:
        nb, ns = bsz, s
    tok = lambda width: pl.BlockSpec((nb, ns, width), lambda b, i: (b, i, 0))
    weights = (w_out, gate_w, ple_w)
    return pl.pallas_call(
        _fox_out_kernel,
        grid=(bsz // nb, s // ns),
        in_specs=[tok(o.shape[-1]), tok(zs.shape[-1]), tok(d),
                  pl.BlockSpec((None, nb, ns, pd), lambda b, i: (layer, b, i, 0))]
                 + [_const_spec(w.shape) for w in weights],
        out_specs=tok(d),
        out_shape=jax.ShapeDtypeStruct((bsz, s, d), F32),
        compiler_params=_params(("arbitrary", "arbitrary")),
        name="fox_out",
    )(o, zs, x, p_all, *weights)


TM_PROMPT = 256
TM_KV = 512
KV_PARTS = 4
TQ_PROMPT = 512
ATTN_HEADS_PER_STEP = 2
PAGES_PER_STEP = 16


def kernel(x_prompt, x_sample, state_conv, cache_k, cache_v, cache_logf, page_table, p_prompt, p_sample,
           a_norm, a_w_in, a_conv_w, a_conv_b, a_ln_g, a_ln_b, a_w_out, kv_norm, kv_w, kv_k_norm, kv_f_bias,
           b_norm, b_w_q, b_q_norm, b_w_out, ple_w, ple_gate_w):
    n_a = a_norm.shape[0]
    n_b = b_norm.shape[0]
    n_heads = kv_f_bias.shape[0]
    e_b = n_heads * HEAD_DIM
    d = x_prompt.shape[-1]
    row = lambda v: v.reshape(1, -1)
    bf = _lane_tiles

    a_w_in16, a_w_out16, b_w_q16, b_w_out16 = bf(a_w_in), bf(a_w_out), bf(b_w_q), bf(b_w_out)
    ple_w16, gate_w16 = bf(ple_w), bf(ple_gate_w)
    w_kv16 = bf(kv_w[:, :2 * e_b])
    w_f16 = bf(jnp.pad(kv_w[:, 2 * e_b:], ((0, 0), (0, LANES - n_heads))))
    f_bias = jnp.pad(row(kv_f_bias), ((0, 0), (0, LANES - n_heads)))
    k_gain = jnp.tile(row(kv_k_norm), (1, n_heads))
    assert e_b % MXU_DEPTH == 0 and MXU_DEPTH % HEAD_DIM == 0
    bd = _block_diag_ones()
    tm = min(TM_PROMPT, x_prompt.shape[1])
    tq = min(TQ_PROMPT, x_prompt.shape[1])

    def trunk(x, p_all, prev, prompt):
        conv_states = []
        if prev is not None:
            prev = jnp.transpose(prev, (0, 2, 1, 3))
        for i in range(n_a):
            x, st = _conv_layer(x, p_all, i, prev, row(a_norm[i]), a_w_in16[i],
                                a_conv_w[i], row(a_conv_b[i]), row(a_ln_g[i]), row(a_ln_b[i]), a_w_out16[i],
                                gate_w16[i], ple_w16[i], tm=tm)
            conv_states.append(st)
        kv = _kv_proj(x, row(kv_norm), w_kv16, w_f16, f_bias, k_gain, bd, n_heads=n_heads, prompt=prompt,
                      tm=min(TM_KV, x.shape[1]))
        k_new, v_new, lf_new = kv[:3]
        for j in range(n_b):
            q_gain = jnp.tile(row(b_q_norm[j]), (1, n_heads)) * (LOG2E * HEAD_DIM ** -0.5)
            if prompt:
                qp, zs = _fox_q(x, kv[3], row(b_norm[j]), b_w_q16[j], q_gain, bd, n_heads=n_heads, prompt=True,
                                tm=min(TM_KV, x.shape[1]))
                o = _attn_prompt(qp, kv[4], kv[5], tq=tq, heads_per_step=ATTN_HEADS_PER_STEP)
            else:
                qn, zs = _fox_q(x, None, row(b_norm[j]), b_w_q16[j], q_gain, bd, n_heads=n_heads, prompt=False, tm=tm)
                o = _attn_decode(qn, k_new, v_new, lf_new, cache_k, cache_v, cache_logf, page_table,
                                 n_heads=n_heads, pages_per_step=min(PAGES_PER_STEP, page_table.shape[1]))
            x = _fox_out(o, zs, x, p_all, n_a + j, b_w_out16[j], gate_w16[n_a + j], ple_w16[n_a + j], tm=tm)
        shape4 = k_new.shape[:2] + (n_heads, HEAD_DIM)
        conv_state = jnp.transpose(jnp.stack(conv_states, axis=0), (0, 2, 1, 3))
        return x, conv_state, k_new.reshape(shape4), v_new.reshape(shape4), lf_new

    y_p, conv_p, k_p, v_p, lf_p = trunk(x_prompt, p_prompt, None, True)
    y_s, conv_s, k_s, v_s, lf_s = trunk(x_sample, p_sample, state_conv, False)
    return (y_p, y_s, conv_p, k_p, v_p, lf_p, conv_s, k_s, v_s, lf_s)
```
